```python
import jax
import jax.numpy as jnp
from jax import lax
import numpy as np

D_MODEL = 1024
BATCH = 4
SEQ = 4096
DEPTH = 2
DEC_BATCH = 128
DEC_SEQ = 4
PAST_LEN = 16384
PAGE_SIZE = 128

HEAD_DIM = 64
ROT_DIM = HEAD_DIM // 4
ROPE_THETA = 500000.0
ATTN_SCALE = HEAD_DIM ** -0.5

MLA_HEADS = 8
MLA_Q_LORA = 256
MLA_KV_LORA = 128
MLA_NOPE = 64
MLA_ROPE = 32
MLA_V = 64
MLA_THETA = 10000.0
MLA_SCALE = (MLA_NOPE + MLA_ROPE) ** -0.5
MLA_CACHE_W = MLA_KV_LORA + MLA_ROPE

NSA_HEADS = 8
NSA_CMP_BLOCK = 32
NSA_CMP_STRIDE = 16
NSA_CMP_HIDDEN = 256
NSA_SEL_BLOCK = 64
NSA_SEL_TOPK = 16
NSA_WINDOW = 512
NSA_FORCE = 1000.0
NSA_CMP_PER = NSA_CMP_BLOCK // NSA_CMP_STRIDE
NSA_SEL_PER = NSA_SEL_BLOCK // NSA_CMP_STRIDE

MOBA_HEADS = 8
MOBA_KV_HEADS = 2
MOBA_BLOCK = 256
MOBA_TOPK = 3

N_BRANCH = 3
BRANCH_WIDTH = MLA_HEADS * MLA_V
D_FF = 3584
N_EXPERTS = 8
TOP_K = 2
EXPERT_FF = 3584
N_DENSE_LAYERS = (DEPTH + 1) // 2
N_MOE_LAYERS = DEPTH // 2

Q_BLOCK = 128
RMS_EPS = 1e-6
NEG = -1e30

N_PAGES = PAST_LEN // PAGE_SIZE
N_POOL_PAGES = (DEC_BATCH * N_PAGES * 5) // 4
WIN_BUF = min(NSA_WINDOW, PAST_LEN)

IN_SPLITS = (MLA_Q_LORA, MLA_KV_LORA, MLA_ROPE,
             NSA_HEADS * HEAD_DIM, 2 * HEAD_DIM, 2 * HEAD_DIM, 2 * HEAD_DIM, 3 * NSA_HEADS,
             MOBA_HEADS * HEAD_DIM, 2 * MOBA_KV_HEADS * HEAD_DIM,
             N_BRANCH * D_MODEL)
N_IN = sum(IN_SPLITS)

kernel_name = 'gated_mla_nsa_moba_decoder_step'


def rms_norm(x, g):
    xf = x.astype(jnp.float32)
    y = xf * lax.rsqrt(jnp.mean(xf * xf, axis=-1, keepdims=True) + RMS_EPS)
    return (y * g.astype(jnp.float32)).astype(x.dtype)


def rope_tables(pos, rot_dim, theta):
    inv = 1.0 / (theta ** (jnp.arange(0, rot_dim, 2, dtype=jnp.float32) / rot_dim))
    ang = pos.astype(jnp.float32)[:, None] * inv[None, :]
    return jnp.cos(ang), jnp.sin(ang)


def apply_rope(x, cos, sin):
    r = 2 * cos.shape[-1]
    xr = x[..., :r].astype(jnp.float32)
    x1, x2 = xr[..., : r // 2], xr[..., r // 2:]
    rot = jnp.concatenate([x1 * cos - x2 * sin, x2 * cos + x1 * sin], axis=-1).astype(x.dtype)
    return jnp.concatenate([rot, x[..., r:]], axis=-1)


def masked_softmax(s, mask):
    s = jnp.where(mask, s.astype(jnp.float32), NEG)
    m = jnp.max(s, axis=-1, keepdims=True)
    e = jnp.where(mask, jnp.exp(s - m), 0.0)
    return e / jnp.maximum(jnp.sum(e, axis=-1, keepdims=True), 1e-30)


def pad_rows(a, n):
    return jnp.pad(a, [(0, n - a.shape[0])] + [(0, 0)] * (a.ndim - 1))


def project(h, pos, w_in, q_norm, kv_norm, w_uq, w_uk):
    n, s, _ = h.shape
    z = jnp.einsum('nsd,de->nse', h, w_in)
    points = [int(v) for v in np.cumsum(IN_SPLITS)[:-1]]
    cq, ckv, kr, qb, kv_cmp, kv_sl, kv_win, gb, qc, kv_moba, gm = jnp.split(z, points, axis=-1)
    cos_m, sin_m = rope_tables(pos, MLA_ROPE, MLA_THETA)
    cos_p, sin_p = rope_tables(pos, ROT_DIM, ROPE_THETA)
    q = jnp.einsum('nsc,chd->nshd', rms_norm(cq, q_norm), w_uq)
    q_lat = jnp.einsum('nshd,chd->nshc', q[..., :MLA_NOPE], w_uk)
    q_rope = apply_rope(q[..., MLA_NOPE:], cos_m[:, None], sin_m[:, None])
    q_mla = jnp.concatenate([q_lat, q_rope], axis=-1)
    mla_rows = jnp.concatenate([rms_norm(ckv, kv_norm), apply_rope(kr, cos_m, sin_m)], axis=-1)

    def kv_pair(t):
        kv = t.reshape(n, s, 2, HEAD_DIM)
        return jnp.stack([apply_rope(kv[:, :, 0], cos_p, sin_p), kv[:, :, 1]], axis=2)

    q_nsa = apply_rope(qb.reshape(n, s, NSA_HEADS, HEAD_DIM), cos_p[:, None], sin_p[:, None])
    g_nsa = jax.nn.sigmoid(gb).reshape(n, s, 3, NSA_HEADS)
    q_moba = apply_rope(qc.reshape(n, s, MOBA_HEADS, HEAD_DIM), cos_p[:, None], sin_p[:, None])
    kvm = kv_moba.reshape(n, s, 2, MOBA_KV_HEADS, HEAD_DIM)
    moba_rows = jnp.stack([apply_rope(kvm[:, :, 0], cos_p[:, None], sin_p[:, None]), kvm[:, :, 1]], axis=2)
    g_merge = jax.nn.sigmoid(gm).reshape(n, s, N_BRANCH, D_MODEL)
    queries = (q_mla, q_nsa, g_nsa, q_moba)
    rows = (mla_rows, kv_pair(kv_cmp), kv_pair(kv_sl), kv_pair(kv_win), moba_rows)
    return queries, rows, g_merge


def nsa_compress(cmp_kv, cmp_pos, w1, w2):
    length = cmp_kv.shape[0]
    n_cmp = (length - NSA_CMP_BLOCK) // NSA_CMP_STRIDE + 1
    n_chunks = n_cmp + NSA_CMP_PER - 1
    chunks = cmp_kv[: n_chunks * NSA_CMP_STRIDE].reshape(n_chunks, NSA_CMP_STRIDE, 2, HEAD_DIM)
    win = jnp.concatenate([chunks[j: j + n_cmp] for j in range(NSA_CMP_PER)], axis=1)
    hid = jax.nn.gelu(jnp.einsum('nbcd,cbde->nce', win + cmp_pos, w1))
    return jnp.einsum('nce,ced->ncd', hid, w2)


def seq_context(cmp_kv, sel_kv, moba_kv, cmp_pos, w1, w2):
    length = cmp_kv.shape[0]
    cmp_tok = nsa_compress(cmp_kv, cmp_pos, w1, w2)
    n_sel = max(-(-length // NSA_SEL_BLOCK), NSA_SEL_TOPK)
    sel_blk = pad_rows(sel_kv, n_sel * NSA_SEL_BLOCK).reshape((n_sel, NSA_SEL_BLOCK) + sel_kv.shape[1:])
    n_blk = max(-(-length // MOBA_BLOCK), MOBA_TOPK)
    moba_blk = pad_rows(moba_kv, n_blk * MOBA_BLOCK).reshape((n_blk, MOBA_BLOCK) + moba_kv.shape[1:])
    means = jnp.mean(moba_blk[:, :, 0].astype(jnp.float32), axis=1)
    return cmp_tok, sel_blk, moba_blk, means


def mla_attend(q_cat, qpos, rows):
    kpos = jnp.arange(rows.shape[0])
    s = jnp.einsum('qhe,le->qhl', q_cat, rows) * MLA_SCALE
    p = masked_softmax(s, (kpos[None, :] <= qpos[:, None])[:, None, :])
    return jnp.einsum('qhl,lc->qhc', p.astype(rows.dtype), rows[:, :MLA_KV_LORA])


def nsa_attend(q, gate, qpos, cmp_tok, sel_blk, win_kv, win_pos):
    nq = q.shape[0]
    n_cmp = cmp_tok.shape[0]
    n_sel = sel_blk.shape[0]
    cmp_end = jnp.arange(n_cmp) * NSA_CMP_STRIDE + (NSA_CMP_BLOCK - 1)
    m_cmp = (cmp_end[None, :] <= qpos[:, None])[:, None, :]
    p_cmp = masked_softmax(jnp.einsum('qhd,nd->qhn', q, cmp_tok[:, 0]) * ATTN_SCALE, m_cmp)
    o_cmp = jnp.einsum('qhn,nd->qhd', p_cmp.astype(q.dtype), cmp_tok[:, 1])
    a, r = NSA_SEL_PER, NSA_CMP_PER
    imp = jnp.sum(p_cmp, axis=1)
    tgt = n_sel * a + a + r
    imp = jnp.pad(imp, ((0, 0), (r - 1, tgt - (r - 1) - n_cmp)))
    p_slc = jnp.stack([imp[:, j: j + n_sel * a: a] for j in range(a + r - 1)], 0).sum(0)
    cur = qpos // NSA_SEL_BLOCK
    jb = jnp.arange(n_sel)[None, :]
    forced = (jb == 0) | (jb == cur[:, None]) | (jb == cur[:, None] - 1)
    score = jnp.where(jb <= cur[:, None], p_slc + NSA_FORCE * forced.astype(jnp.float32), NEG)
    _, idx = lax.top_k(score, NSA_SEL_TOPK)
    valid = idx <= cur[:, None]
    kv = sel_blk[idx]
    kpos = idx[..., None] * NSA_SEL_BLOCK + jnp.arange(NSA_SEL_BLOCK)
    m_sel = (valid[..., None] & (kpos <= qpos[:, None, None])).reshape(nq, 1, -1)
    kv = kv.reshape(nq, -1, 2, HEAD_DIM)
    p_sel = masked_softmax(jnp.einsum('qhd,qmd->qhm', q, kv[:, :, 0]) * ATTN_SCALE, m_sel)
    o_sel = jnp.einsum('qhm,qmd->qhd', p_sel.astype(q.dtype), kv[:, :, 1])
    dist = qpos[:, None] - win_pos[None, :]
    m_win = ((dist >= 0) & (dist <= NSA_WINDOW) & (win_pos[None, :] >= 0))[:, None, :]
    p_win = masked_softmax(jnp.einsum('qhd,kd->qhk', q, win_kv[:, 0]) * ATTN_SCALE, m_win)
    o_win = jnp.einsum('qhk,kd->qhd', p_win.astype(q.dtype), win_kv[:, 1])
    return gate[:, 0, :, None] * o_cmp + gate[:, 1, :, None] * o_sel + gate[:, 2, :, None] * o_win


def moba_attend(q, qpos, blk, means):
    nq = q.shape[0]
    grp = MOBA_HEADS // MOBA_KV_HEADS
    hk = jnp.arange(MOBA_HEADS) // grp
    cur = qpos // MOBA_BLOCK
    jb = jnp.arange(blk.shape[0])
    gsc = jnp.einsum('qhd,nhd->qhn', q.astype(jnp.float32), means[:, hk])
    gsc = jnp.where(jb[None, None, :] < cur[:, None, None], gsc, NEG)
    _, idx = lax.top_k(gsc, MOBA_TOPK)
    valid = idx < cur[:, None, None]
    kv_sel = blk[idx, :, :, hk[None, :, None], :]
    own_pos = (cur * MOBA_BLOCK)[:, None] + jnp.arange(MOBA_BLOCK)
    kv_own = blk.reshape((-1,) + blk.shape[2:])[own_pos]
    qg = q.reshape(nq, MOBA_KV_HEADS, grp, HEAD_DIM)
    s_sel = jnp.einsum('qhd,qhjbd->qhjb', q, kv_sel[..., 0, :]).reshape(nq, MOBA_HEADS, -1)
    s_own = jnp.einsum('qkgd,qbkd->qkgb', qg, kv_own[:, :, 0]).reshape(nq, MOBA_HEADS, MOBA_BLOCK)
    m_sel = jnp.broadcast_to(valid[..., None], valid.shape + (MOBA_BLOCK,)).reshape(nq, MOBA_HEADS, -1)
    m_own = jnp.broadcast_to((own_pos <= qpos[:, None])[:, None, :], (nq, MOBA_HEADS, MOBA_BLOCK))
    p = masked_softmax(jnp.concatenate([s_sel, s_own], axis=-1) * ATTN_SCALE,
                       jnp.concatenate([m_sel, m_own], axis=-1)).astype(q.dtype)
    n_s = MOBA_TOPK * MOBA_BLOCK
    p_sel = p[..., :n_s].reshape(nq, MOBA_HEADS, MOBA_TOPK, MOBA_BLOCK)
    p_own = p[..., n_s:].reshape(nq, MOBA_KV_HEADS, grp, MOBA_BLOCK)
    o_sel = jnp.einsum('qhjb,qhjbd->qhd', p_sel, kv_sel[..., 1, :])
    o_own = jnp.einsum('qkgb,qbkd->qkgd', p_own, kv_own[:, :, 1]).reshape(nq, MOBA_HEADS, HEAD_DIM)
    return o_sel + o_own


def attend_block(q_mla, q_nsa, g_nsa, q_moba, qpos, mla_rows, ctx, win_kv, win_pos):
    cmp_tok, sel_blk, moba_blk, means = ctx
    o_lat = mla_attend(q_mla, qpos, mla_rows)
    o_nsa = nsa_attend(q_nsa, g_nsa, qpos, cmp_tok, sel_blk, win_kv, win_pos)
    o_moba = moba_attend(q_moba, qpos, moba_blk, means)
    return o_lat, o_nsa, o_moba


def prompt_mix(q_mla, q_nsa, g_nsa, q_moba, mla_rows, cmp_kv, sel_kv, win_kv, moba_kv, cmp_pos, w1, w2):
    seq = q_mla.shape[1]
    n_qb = seq // Q_BLOCK

    def per_seq(a):
        qm, qn, gn, qc, mr, ck, sk, wk, mk = a
        ctx = seq_context(ck, sk, mk, cmp_pos, w1, w2)
        wk_pad = jnp.pad(wk, ((NSA_WINDOW, 0), (0, 0), (0, 0)))

        def per_block(i):
            s0 = i * Q_BLOCK
            qpos = s0 + jnp.arange(Q_BLOCK)
            band = lax.dynamic_slice_in_dim(wk_pad, s0, NSA_WINDOW + Q_BLOCK, 0)
            bpos = s0 - NSA_WINDOW + jnp.arange(NSA_WINDOW + Q_BLOCK)
            return attend_block(lax.dynamic_slice_in_dim(qm, s0, Q_BLOCK, 0),
                                lax.dynamic_slice_in_dim(qn, s0, Q_BLOCK, 0),
                                lax.dynamic_slice_in_dim(gn, s0, Q_BLOCK, 0),
                                lax.dynamic_slice_in_dim(qc, s0, Q_BLOCK, 0),
                                qpos, mr, ctx, band, bpos)

        o1, o2, o3 = lax.map(per_block, jnp.arange(n_qb))
        return (o1.reshape((seq,) + o1.shape[2:]), o2.reshape((seq,) + o2.shape[2:]),
                o3.reshape((seq,) + o3.shape[2:]))

    return lax.map(per_seq, (q_mla, q_nsa, g_nsa, q_moba, mla_rows, cmp_kv, sel_kv, win_kv, moba_kv))


def sample_mix(q_mla, q_nsa, g_nsa, q_moba, mla_rows, cmp_kv, sel_kv, win_kv, moba_kv,
               layer, cache_mla, cache_nsa_cmp, cache_nsa_sel, cache_moba, page_table, win_state,
               cmp_pos, w1, w2):
    n_new = q_mla.shape[1]
    qpos = PAST_LEN + jnp.arange(n_new)
    wb = win_state.shape[1]
    wpos = PAST_LEN - wb + jnp.arange(wb + n_new)

    def with_past(pool, pages, new):
        past = pool[layer, pages]
        return jnp.concatenate([past.reshape((-1,) + past.shape[2:]), new], axis=0)

    def per_seq(a):
        qm, qn, gn, qc, mr, ck, sk, wk, mk, pages, ws = a
        ctx = seq_context(with_past(cache_nsa_cmp, pages, ck), with_past(cache_nsa_sel, pages, sk),
                          with_past(cache_moba, pages, mk), cmp_pos, w1, w2)
        win = jnp.concatenate([ws, wk], axis=0)
        return attend_block(qm, qn, gn, qc, qpos, with_past(cache_mla, pages, mr), ctx, win, wpos)

    return lax.map(per_seq, (q_mla, q_nsa, g_nsa, q_moba, mla_rows, cmp_kv, sel_kv, win_kv, moba_kv,
                             page_table, win_state))


def merge_branches(o_lat, o_nsa, o_moba, g_merge, w_uv, w_branch, w_out):
    n, s = o_lat.shape[:2]
    o_a = jnp.einsum('nshc,chv->nshv', o_lat, w_uv).reshape(n, s, BRANCH_WIDTH)
    br = jnp.stack([o_a, o_nsa.reshape(n, s, BRANCH_WIDTH), o_moba.reshape(n, s, BRANCH_WIDTH)], axis=2)
    up = jnp.einsum('nsbw,bwd->nsbd', br, w_branch)
    return jnp.einsum('nsd,de->nse', jnp.sum(g_merge * up, axis=2), w_out)


def swiglu(h, w_gu, w_d):
    g, u = jnp.split(jnp.einsum('nsd,df->nsf', h, w_gu), 2, axis=-1)
    return jnp.einsum('nsf,fd->nsd', jax.nn.silu(g) * u, w_d)


def moe(h, router, w_gu, w_d):
    logits = jnp.einsum('nsd,de->nse', h, router).astype(jnp.float32)
    top_v, top_i = lax.top_k(logits, TOP_K)
    top_w = jax.nn.softmax(top_v, axis=-1)
    gate = jnp.sum(jax.nn.one_hot(top_i, N_EXPERTS, dtype=jnp.float32) * top_w[..., None], axis=-2).astype(h.dtype)
    y = jnp.zeros_like(h)
    for e in range(N_EXPERTS):
        y = y + gate[..., e:e + 1] * swiglu(h, w_gu[e], w_d[e])
    return y


def setup_inputs(seed: int = 0) -> dict:
    key = jax.random.key(seed)
    ks = jax.random.split(key, 32)

    def nrm(i, shape, scale):
        return jax.random.normal(ks[i], shape, jnp.float32) * scale

    pool = (DEPTH, N_POOL_PAGES, PAGE_SIZE)
    perm = jax.random.permutation(ks[7], N_POOL_PAGES)
    page_table = perm[: DEC_BATCH * N_PAGES].reshape(DEC_BATCH, N_PAGES).astype(jnp.int32)
    return {
        'x_prompt': nrm(0, (BATCH, SEQ, D_MODEL), 1.0),
        'x_sample': nrm(1, (DEC_BATCH, DEC_SEQ, D_MODEL), 1.0),
        'cache_mla': nrm(2, pool + (MLA_CACHE_W,), 1.0),
        'cache_nsa_cmp': nrm(3, pool + (2, HEAD_DIM), 1.0),
        'cache_nsa_sel': nrm(4, pool + (2, HEAD_DIM), 1.0),
        'state_nsa_win': nrm(5, (DEPTH, DEC_BATCH, WIN_BUF, 2, HEAD_DIM), 1.0),
        'cache_moba': nrm(6, pool + (2, MOBA_KV_HEADS, HEAD_DIM), 1.0),
        'page_table': page_table,
        'norm_mix': 1.0 + nrm(8, (DEPTH, D_MODEL), 0.02),
        'w_in': nrm(9, (DEPTH, D_MODEL, N_IN), D_MODEL ** -0.5),
        'mla_q_norm': 1.0 + nrm(10, (DEPTH, MLA_Q_LORA), 0.02),
        'mla_kv_norm': 1.0 + nrm(11, (DEPTH, MLA_KV_LORA), 0.02),
        'mla_w_uq': nrm(12, (DEPTH, MLA_Q_LORA, MLA_HEADS, MLA_NOPE + MLA_ROPE), MLA_Q_LORA ** -0.5),
        'mla_w_uk': nrm(13, (DEPTH, MLA_KV_LORA, MLA_HEADS, MLA_NOPE), MLA_KV_LORA ** -0.5),
        'mla_w_uv': nrm(14, (DEPTH, MLA_KV_LORA, MLA_HEADS, MLA_V), MLA_KV_LORA ** -0.5),
        'nsa_cmp_pos': nrm(15, (DEPTH, NSA_CMP_BLOCK, 2, HEAD_DIM), 0.1),
        'nsa_cmp_w1': nrm(16, (DEPTH, 2, NSA_CMP_BLOCK, HEAD_DIM, NSA_CMP_HIDDEN), (NSA_CMP_BLOCK * HEAD_DIM) ** -0.5),
        'nsa_cmp_w2': nrm(17, (DEPTH, 2, NSA_CMP_HIDDEN, HEAD_DIM), NSA_CMP_HIDDEN ** -0.5),
        'w_branch': nrm(18, (DEPTH, N_BRANCH, BRANCH_WIDTH, D_MODEL), BRANCH_WIDTH ** -0.5),
        'w_out': nrm(19, (DEPTH, D_MODEL, D_MODEL), D_MODEL ** -0.5),
        'norm_ffn': 1.0 + nrm(20, (DEPTH, D_MODEL), 0.02),
        'ffn_w_gate_up': nrm(21, (N_DENSE_LAYERS, D_MODEL, 2 * D_FF), D_MODEL ** -0.5),
        'ffn_w_down': nrm(22, (N_DENSE_LAYERS, D_FF, D_MODEL), D_FF ** -0.5),
        'moe_router': nrm(23, (N_MOE_LAYERS, D_MODEL, N_EXPERTS), D_MODEL ** -0.5),
        'moe_w_gate_up': nrm(24, (N_MOE_LAYERS, N_EXPERTS, D_MODEL, 2 * EXPERT_FF), D_MODEL ** -0.5),
        'moe_w_down': nrm(25, (N_MOE_LAYERS, N_EXPERTS, EXPERT_FF, D_MODEL), EXPERT_FF ** -0.5),
        'norm_final': 1.0 + nrm(26, (D_MODEL,), 0.02),
    }


def reference(x_prompt, x_sample, cache_mla, cache_nsa_cmp, cache_nsa_sel, state_nsa_win, cache_moba,
              page_table, norm_mix, w_in, mla_q_norm, mla_kv_norm, mla_w_uq, mla_w_uk, mla_w_uv,
              nsa_cmp_pos, nsa_cmp_w1, nsa_cmp_w2, w_branch, w_out, norm_ffn, ffn_w_gate_up, ffn_w_down,
              moe_router, moe_w_gate_up, moe_w_down, norm_final):
    pos_p = jnp.arange(x_prompt.shape[1])
    pos_s = PAST_LEN + jnp.arange(x_sample.shape[1])
    xp, xs = x_prompt, x_sample
    mla_p, mla_s, cmp_p, cmp_s, sel_p, sel_s = [], [], [], [], [], []
    win_p, win_s, moba_p, moba_s = [], [], [], []
    for l in range(DEPTH):
        qp, rp, gp = project(rms_norm(xp, norm_mix[l]), pos_p, w_in[l], mla_q_norm[l], mla_kv_norm[l],
                             mla_w_uq[l], mla_w_uk[l])
        qs, rs, gs = project(rms_norm(xs, norm_mix[l]), pos_s, w_in[l], mla_q_norm[l], mla_kv_norm[l],
                             mla_w_uq[l], mla_w_uk[l])
        op = prompt_mix(*qp, *rp, nsa_cmp_pos[l], nsa_cmp_w1[l], nsa_cmp_w2[l])
        osm = sample_mix(*qs, *rs, l, cache_mla, cache_nsa_cmp, cache_nsa_sel, cache_moba, page_table,
                         state_nsa_win[l], nsa_cmp_pos[l], nsa_cmp_w1[l], nsa_cmp_w2[l])
        xp = xp + merge_branches(*op, gp, mla_w_uv[l], w_branch[l], w_out[l])
        xs = xs + merge_branches(*osm, gs, mla_w_uv[l], w_branch[l], w_out[l])
        hp = rms_norm(xp, norm_ffn[l])
        hs = rms_norm(xs, norm_ffn[l])
        i = l // 2
        if l % 2 == 0:
            xp = xp + swiglu(hp, ffn_w_gate_up[i], ffn_w_down[i])
            xs = xs + swiglu(hs, ffn_w_gate_up[i], ffn_w_down[i])
        else:
            xp = xp + moe(hp, moe_router[i], moe_w_gate_up[i], moe_w_down[i])
            xs = xs + moe(hs, moe_router[i], moe_w_gate_up[i], moe_w_down[i])
        mla_p.append(rp[0])
        mla_s.append(rs[0])
        cmp_p.append(rp[1])
        cmp_s.append(rs[1])
        sel_p.append(rp[2])
        sel_s.append(rs[2])
        seq = rp[3].shape[1]
        win_p.append(rp[3][:, seq - min(NSA_WINDOW, seq):])
        win_s.append(jnp.concatenate([state_nsa_win[l], rs[3]], axis=1)[:, rs[3].shape[1]:])
        moba_p.append(rp[4])
        moba_s.append(rs[4])
    y_prompt = rms_norm(xp, norm_final)
    y_sample = rms_norm(xs, norm_final)
    return (y_prompt, y_sample, jnp.stack(mla_p), jnp.stack(mla_s), jnp.stack(cmp_p), jnp.stack(cmp_s),
            jnp.stack(sel_p), jnp.stack(sel_s), jnp.stack(win_p), jnp.stack(win_s),
            jnp.stack(moba_p), jnp.stack(moba_s))
```

```python
import functools

import numpy as np
import jax
import jax.numpy as jnp
from jax import lax
from jax.experimental import pallas as pl
from jax.experimental.pallas import tpu as pltpu

F32 = jnp.float32
BF16 = jnp.bfloat16
NEG = -1e30
RMS_EPS = 1e-6

N_HEADS = 8
HEAD_DIM = 64
LANE = 128
ROT_HALF = 8
ROPE_THETA = 500000.0
ATTN_SCALE = HEAD_DIM ** -0.5
MLA_Q_LORA = 256
MLA_KV_LORA = 128
MLA_NOPE = 64
MLA_ROPE = 32
MLA_THETA = 10000.0
MLA_SCALE = (MLA_NOPE + MLA_ROPE) ** -0.5
MLA_W = MLA_KV_LORA + MLA_ROPE
MLA_WP = 256
CMP_BLOCK = 32
CMP_STRIDE = 16
CMP_HIDDEN = 256
SEL_BLOCK = 64
SEL_TOPK = 16
NSA_WINDOW = 512
NSA_FORCE = 1000.0
MOBA_BLOCK = 256
MOBA_TOPK = 3
MOBA_KVH = 2
N_EXPERTS = 8
PAGE = 128

VMEM_LIMIT = 56 * 1024 * 1024

C_CQ, C_CKV, C_KR, C_QB, C_CMP, C_SEL, C_WIN, C_GB, C_QC, C_KVM, C_END = (
    0, 256, 384, 512, 1536, 1664, 1792, 1920, 2048, 3072, 3328)


def _dot(a, b):
    return jnp.dot(a, b, preferred_element_type=F32)


def _dot_nt(a, b):
    return lax.dot_general(a, b, (((1,), (1,)), ((), ())), preferred_element_type=F32)


def _rms(x, g):
    return x * lax.rsqrt(jnp.mean(x * x, axis=-1, keepdims=True) + RMS_EPS) * g


def _rope(v, c, s1, s2, half):
    return v * c + pltpu.roll(v, LANE - half, 1) * s1 + pltpu.roll(v, half, 1) * s2


def _topk_mask(score, k, jbf):
    sel = jnp.zeros(score.shape, F32)
    sc = score
    for _ in range(k):
        mx = jnp.max(sc, axis=-1, keepdims=True)
        idx = jnp.min(jnp.where(sc == mx, jbf, 1e9), axis=-1, keepdims=True)
        pick = jbf == idx
        sel = jnp.where(pick, 1.0, sel)
        sc = jnp.where(pick, -jnp.inf, sc)
    return sel


def _params(sem):
    return pltpu.CompilerParams(dimension_semantics=sem, vmem_limit_bytes=VMEM_LIMIT)


def _proj_kernel(x_ref, g_ref, wp_ref, qn_ref, kvn_ref, wuqn_ref, wuqr_ref, wuk_ref,
                 cd_ref, s1d_ref, s2d_ref, cm_ref, s1m_ref, s2m_ref,
                 qmla_ref, mlaf_ref, mlab_ref, qnsa_ref, cmpf_ref, self_ref, selb_ref, winf_ref, winb_ref,
                 gate_ref, qmoba_ref, mobaf_ref, mobab_ref):
    h = _rms(x_ref[...], g_ref[...]).astype(BF16)

    def seg(a, b):
        return _dot(h, wp_ref[:, a:b])

    lo = lax.broadcasted_iota(jnp.int32, (1, LANE), 1) < HEAD_DIM
    cd, s1d, s2d = cd_ref[...], s1d_ref[...], s2d_ref[...]
    cs, s1s, s2s = jnp.where(lo, cd, 1.0), jnp.where(lo, s1d, 0.0), jnp.where(lo, s2d, 0.0)
    cm, s1m, s2m = cm_ref[...], s1m_ref[...], s2m_ref[...]

    cqn = _rms(seg(C_CQ, C_CKV), qn_ref[...]).astype(BF16)
    qlat = _dot(_dot(cqn, wuqn_ref[...]).astype(BF16), wuk_ref[...])
    qr = _dot(cqn, wuqr_ref[...])
    for hh in range(N_HEADS):
        blk = slice(hh * LANE, (hh + 1) * LANE)
        qmla_ref[:, hh * MLA_WP:hh * MLA_WP + LANE] = qlat[:, blk].astype(BF16)
        qmla_ref[:, hh * MLA_WP + LANE:(hh + 1) * MLA_WP] = _rope(qr[:, blk], cm, s1m, s2m, MLA_ROPE // 2).astype(BF16)

    ckvn = _rms(seg(C_CKV, C_KR), kvn_ref[...])
    kr = _rope(seg(C_KR, C_QB), cm, s1m, s2m, MLA_ROPE // 2)
    mlaf_ref[:, 0:MLA_KV_LORA] = ckvn
    mlaf_ref[:, MLA_KV_LORA:MLA_W] = kr[:, 0:MLA_ROPE]
    mlab_ref[:, 0:LANE] = ckvn.astype(BF16)
    mlab_ref[:, LANE:MLA_WP] = kr.astype(BF16)

    qb = seg(C_QB, C_CMP)
    for hh in range(N_HEADS):
        blk = slice(hh * LANE, (hh + 1) * LANE)
        qnsa_ref[:, blk] = _rope(qb[:, blk], cd, s1d, s2d, ROT_HALF).astype(BF16)
    cmpf_ref[...] = _rope(seg(C_CMP, C_SEL), cs, s1s, s2s, ROT_HALF)
    sl = _rope(seg(C_SEL, C_WIN), cs, s1s, s2s, ROT_HALF)
    self_ref[...] = sl
    selb_ref[...] = sl.astype(BF16)
    wn = _rope(seg(C_WIN, C_GB), cs, s1s, s2s, ROT_HALF)
    winf_ref[...] = wn
    winb_ref[...] = wn.astype(BF16)
    gate_ref[...] = jax.nn.sigmoid(seg(C_GB, C_QC))

    qc = seg(C_QC, C_KVM)
    for hh in range(N_HEADS):
        blk = slice(hh * LANE, (hh + 1) * LANE)
        qmoba_ref[:, blk] = _rope(qc[:, blk], cd, s1d, s2d, ROT_HALF).astype(BF16)
    kvm = seg(C_KVM, C_END)
    km = _rope(kvm[:, 0:LANE], cd, s1d, s2d, ROT_HALF)
    mobaf_ref[:, 0:LANE] = km
    mobaf_ref[:, LANE:2 * LANE] = kvm[:, LANE:2 * LANE]
    mobab_ref[:, 0:LANE] = km.astype(BF16)
    mobab_ref[:, LANE:2 * LANE] = kvm[:, LANE:2 * LANE].astype(BF16)


def _project(x, g, wl, tabs, tm):
    t, d = x.shape
    row = lambda w: pl.BlockSpec((tm, w), lambda i: (i, 0))
    full = lambda a: pl.BlockSpec(a.shape, lambda i: (0,) * a.ndim)
    consts = (g, wl['wp'], wl['qn'], wl['kvn'], wl['wuqn'], wl['wuqr'], wl['wuk'])
    outs = [(N_HEADS * MLA_WP, BF16), (MLA_W, F32), (MLA_WP, BF16), (N_HEADS * LANE, BF16), (LANE, F32),
            (LANE, F32), (LANE, BF16), (LANE, F32), (LANE, BF16), (LANE, F32), (N_HEADS * LANE, BF16),
            (2 * LANE, F32), (2 * LANE, BF16)]
    return pl.pallas_call(
        _proj_kernel,
        grid=(t // tm,),
        in_specs=[row(d)] + [full(a) for a in consts] + [row(LANE)] * 6,
        out_specs=[row(w) for w, _ in outs],
        out_shape=[jax.ShapeDtypeStruct((t, w), dt) for w, dt in outs],
        compiler_params=_params(("parallel",)),
        name="proj",
    )(x, *consts, *tabs)


def _kv_range(i, tq, tk, mode):
    hi = (i * tq + tq - 1) // tk
    lo = jnp.maximum(i * tq - NSA_WINDOW, 0) // tk if mode == 'win' else 0
    return lo, hi


def _attn_kernel(*refs, mode, tq, tk, kw, voff, scale, nk):
    if mode in ('sel', 'moba'):
        q_ref, kv_ref, sel_ref, o_ref, qs_ref, m_ref, l_ref, acc_ref, sels_ref = refs
    else:
        q_ref, kv_ref, o_ref, qs_ref, m_ref, l_ref, acc_ref = refs
    i = pl.program_id(1)
    j = pl.program_id(2)
    lo, hi = _kv_range(i, tq, tk, mode)

    @pl.when(j == 0)
    def _():
        for hh in range(N_HEADS):
            qs_ref[hh * tq:(hh + 1) * tq, :] = q_ref[:, hh * kw:(hh + 1) * kw]
            if mode == 'moba':
                sels_ref[hh * tq:(hh + 1) * tq, :] = sel_ref[:, hh * LANE:(hh + 1) * LANE]
        m_ref[...] = jnp.full(m_ref.shape, NEG, F32)
        l_ref[...] = jnp.zeros(l_ref.shape, F32)
        acc_ref[...] = jnp.zeros(acc_ref.shape, F32)

    @pl.when(lo + j <= hi)
    def _():
        kb = lo + j
        kpos = kb * tk + lax.broadcasted_iota(jnp.int32, (1, tk), 1)
        qpos = i * tq + lax.broadcasted_iota(jnp.int32, (tq, 1), 0)
        s = _dot_nt(qs_ref[...], kv_ref[:, 0:kw]) * scale
        s3 = s.reshape(N_HEADS, tq, tk)
        causal = kpos <= qpos
        if mode == 'mla':
            mask = causal[None]
        elif mode == 'win':
            mask = (causal & (qpos - kpos <= NSA_WINDOW))[None]
        else:
            shift = 6 if mode == 'sel' else 8
            expand = jnp.where((kpos >> shift) == lax.broadcasted_iota(jnp.int32, (LANE, 1), 0), 1.0, 0.0).astype(BF16)
            if mode == 'sel':
                mask = ((_dot(sel_ref[...], expand) > 0.5) & causal)[None]
            else:
                own = causal & ((kpos >> shift) == (qpos >> shift))
                mask = (_dot(sels_ref[...], expand) > 0.5).reshape(N_HEADS, tq, tk) | own[None]
        s3 = jnp.where(mask, s3, NEG)
        m_old = m_ref[...]
        m_new = jnp.maximum(m_old, jnp.max(s3, axis=-1, keepdims=True))
        p3 = jnp.where(mask, jnp.exp(s3 - m_new), 0.0)
        alpha = jnp.exp(m_old - m_new)
        l_ref[...] = alpha * l_ref[...] + jnp.sum(p3, axis=-1, keepdims=True)
        pv = _dot(p3.reshape(N_HEADS * tq, tk).astype(BF16), kv_ref[:, voff:voff + LANE])
        acc_ref[...] = alpha * acc_ref[...] + pv.reshape(N_HEADS, tq, LANE)
        m_ref[...] = m_new

    @pl.when(j == nk - 1)
    def _():
        o = acc_ref[...] / jnp.maximum(l_ref[...], 1e-30)
        for hh in range(N_HEADS):
            o_ref[:, hh * LANE:(hh + 1) * LANE] = o[hh]


def _prompt_attention(q, kv, sel, *, mode, n_seq, seq, tq=128, tk=512):
    tk = min(tk, seq)
    kw = q.shape[1] // N_HEADS
    wkv = kv.shape[1]
    voff = LANE if mode == 'moba' else 0
    scale = MLA_SCALE if mode == 'mla' else ATTN_SCALE
    nq, nkb = seq // tq, seq // tk
    nk = min(nkb, (NSA_WINDOW + tq - 2) // tk + 2) if mode == 'win' else nkb

    def kv_map(b, i, j):
        lo, hi = _kv_range(i, tq, tk, mode)
        return (b * nkb + jnp.minimum(lo + j, hi), 0)

    q_map = lambda b, i, j: (b * nq + i, 0)
    in_specs = [pl.BlockSpec((tq, N_HEADS * kw), q_map), pl.BlockSpec((tk, wkv), kv_map)]
    args = [q, kv]
    scratch = [pltpu.VMEM((N_HEADS * tq, kw), BF16), pltpu.VMEM((N_HEADS, tq, 1), F32),
               pltpu.VMEM((N_HEADS, tq, 1), F32), pltpu.VMEM((N_HEADS, tq, LANE), F32)]
    if mode in ('sel', 'moba'):
        in_specs.append(pl.BlockSpec((tq, sel.shape[1]), q_map))
        args.append(sel)
        scratch.append(pltpu.VMEM((N_HEADS * tq, LANE), BF16))
    return pl.pallas_call(
        functools.partial(_attn_kernel, mode=mode, tq=tq, tk=tk, kw=kw, voff=voff, scale=scale, nk=nk),
        grid=(n_seq, nq, nk),
        in_specs=in_specs,
        out_specs=pl.BlockSpec((tq, N_HEADS * LANE), q_map),
        out_shape=jax.ShapeDtypeStruct((n_seq * seq, N_HEADS * LANE), F32),
        scratch_shapes=scratch,
        compiler_params=_params(("parallel", "parallel", "arbitrary")),
        name="attn_" + mode,
    )(*args)


def _compress(zla, zlb, w1a_ref, w1b_ref, w2_ref):
    a = _dot(zla, w1a_ref[...])
    b = _dot(zlb, w1b_ref[...])
    hid = jax.nn.gelu(a + pltpu.roll(b, b.shape[0] - 1, 0))
    return _dot(hid.astype(BF16), w2_ref[...])


def _cmp_attend(qs, tok, qpos, n_cmp, head_axis):
    ncp = tok.shape[0]
    rows = qs.shape[0]
    nq = rows // N_HEADS
    shape3 = (N_HEADS, nq, ncp) if head_axis == 0 else (nq, N_HEADS, ncp)
    s3 = (_dot_nt(qs, tok) * ATTN_SCALE).reshape(shape3)
    nidx = lax.broadcasted_iota(jnp.int32, (1, 1, ncp), 2)
    mask = ((nidx * CMP_STRIDE + (CMP_BLOCK - 1)) <= qpos) & (nidx < n_cmp)
    s3 = jnp.where(mask, s3, NEG)
    e = jnp.where(mask, jnp.exp(s3 - jnp.max(s3, axis=-1, keepdims=True)), 0.0)
    p3 = e / jnp.maximum(jnp.sum(e, axis=-1, keepdims=True), 1e-30)
    o = _dot(p3.reshape(rows, ncp).astype(BF16), tok)
    return o, jnp.sum(p3, axis=head_axis)


def _select_blocks(imp, mslc_ref, cur):
    pslc = jnp.dot(imp, mslc_ref[...], precision=lax.Precision.HIGHEST, preferred_element_type=F32)
    jb = lax.broadcasted_iota(jnp.int32, (1, pslc.shape[1]), 1)
    forced = (jb == 0) | (jb == cur) | (jb == cur - 1)
    score = jnp.where(jb <= cur, pslc + NSA_FORCE * forced.astype(F32), NEG)
    sel = _topk_mask(score, SEL_TOPK, jb.astype(F32))
    return jnp.where(jb <= cur, sel, 0.0)


def _nsa_select_kernel(cmp_ref, q_ref, pa_ref, pb_ref, w1a_ref, w1b_ref, w2_ref, mslc_ref,
                       ocmp_ref, sel_ref, zl_ref, tok_ref, *, tq, n_cmp):
    i = pl.program_id(1)
    nch = zl_ref.shape[0]

    @pl.when(i == 0)
    def _():
        for b in range(CMP_STRIDE):
            zl_ref[:, b * LANE:(b + 1) * LANE] = cmp_ref[pl.ds(b, nch, stride=CMP_STRIDE), :]
        zl = zl_ref[...]
        tok = _compress((zl + pa_ref[...]).astype(BF16), (zl + pb_ref[...]).astype(BF16), w1a_ref, w1b_ref, w2_ref)
        tok_ref[...] = tok.astype(BF16)

    qs = jnp.concatenate([q_ref[:, hh * LANE:(hh + 1) * LANE] for hh in range(N_HEADS)], axis=0)
    qpos = i * tq + lax.broadcasted_iota(jnp.int32, (1, tq, 1), 1)
    o, imp = _cmp_attend(qs, tok_ref[...], qpos, n_cmp, 0)
    for hh in range(N_HEADS):
        ocmp_ref[:, hh * LANE:(hh + 1) * LANE] = o[hh * tq:(hh + 1) * tq]
    sel_ref[...] = _select_blocks(imp, mslc_ref, qpos[0] >> 6).astype(BF16)


def _slc_matrix(n_rows, n_cmp, n_cols, n_sel):
    a, r = SEL_BLOCK // CMP_STRIDE, CMP_BLOCK // CMP_STRIDE
    n = np.arange(n_rows)[:, None]
    j = np.arange(n_cols)[None, :]
    ok = (n >= a * j - (r - 1)) & (n <= a * j + a - 1) & (n < n_cmp) & (j < n_sel)
    return jnp.asarray(ok.astype(np.float32))


def _nsa_select_prompt(cmpf, qnsa, wl, *, n_seq, seq, tq=128):
    nch = seq // CMP_STRIDE
    n_cmp = (seq - CMP_BLOCK) // CMP_STRIDE + 1
    n_sel = max(-(-seq // SEL_BLOCK), SEL_TOPK)
    assert n_sel <= LANE and n_cmp + 1 == nch
    mslc = _slc_matrix(nch, n_cmp, LANE, n_sel)
    nq = seq // tq
    consts = (wl['pa'], wl['pb'], wl['w1a'], wl['w1b'], wl['w2'], mslc)
    full = lambda a: pl.BlockSpec(a.shape, lambda b, i: (0,) * a.ndim)
    q_map = lambda b, i: (b * nq + i, 0)
    return pl.pallas_call(
        functools.partial(_nsa_select_kernel, tq=tq, n_cmp=n_cmp),
        grid=(n_seq, nq),
        in_specs=[pl.BlockSpec((seq, LANE), lambda b, i: (b, 0)), pl.BlockSpec((tq, N_HEADS * LANE), q_map)]
        + [full(a) for a in consts],
        out_specs=[pl.BlockSpec((tq, N_HEADS * LANE), q_map), pl.BlockSpec((tq, LANE), q_map)],
        out_shape=[jax.ShapeDtypeStruct((n_seq * seq, N_HEADS * LANE), F32),
                   jax.ShapeDtypeStruct((n_seq * seq, LANE), BF16)],
        scratch_shapes=[pltpu.VMEM((nch, CMP_STRIDE * LANE), F32), pltpu.VMEM((nch, LANE), BF16)],
        compiler_params=_params(("parallel", "arbitrary")),
        name="nsa_select",
    )(cmpf, qnsa, *consts)


def _moba_select_kernel(rows_ref, q_ref, sel_ref, means_ref, *, tq, n_blk):
    i = pl.program_id(1)

    @pl.when(i == 0)
    def _():
        k = rows_ref[:, 0:LANE]
        means = jnp.sum(k.reshape(n_blk, MOBA_BLOCK, LANE), axis=1) * (1.0 / MOBA_BLOCK)
        means_ref[...] = jnp.zeros(means_ref.shape, BF16)
        means_ref[0:n_blk, :] = means.astype(BF16)

    qs = jnp.concatenate([q_ref[:, hh * LANE:(hh + 1) * LANE] for hh in range(N_HEADS)], axis=0)
    gsc = _dot_nt(qs, means_ref[...])
    rows = N_HEADS * tq
    cur = (i * tq + (lax.broadcasted_iota(jnp.int32, (rows, 1), 0) & (tq - 1))) >> 8
    jb = lax.broadcasted_iota(jnp.int32, (1, LANE), 1)
    sel = _topk_mask(jnp.where(jb < cur, gsc, NEG), MOBA_TOPK, jb.astype(F32))
    sel = jnp.where(jb < cur, sel, 0.0).astype(BF16)
    for hh in range(N_HEADS):
        sel_ref[:, hh * LANE:(hh + 1) * LANE] = sel[hh * tq:(hh + 1) * tq]


def _moba_select_prompt(mobaf, qmoba, *, n_seq, seq, tq=128):
    n_blk = seq // MOBA_BLOCK
    assert seq % MOBA_BLOCK == 0 and MOBA_TOPK <= n_blk <= 16
    nq = seq // tq
    q_map = lambda b, i: (b * nq + i, 0)
    return pl.pallas_call(
        functools.partial(_moba_select_kernel, tq=tq, n_blk=n_blk),
        grid=(n_seq, nq),
        in_specs=[pl.BlockSpec((seq, 2 * LANE), lambda b, i: (b, 0)), pl.BlockSpec((tq, N_HEADS * LANE), q_map)],
        out_specs=pl.BlockSpec((tq, N_HEADS * LANE), q_map),
        out_shape=jax.ShapeDtypeStruct((n_seq * seq, N_HEADS * LANE), BF16),
        scratch_shapes=[pltpu.VMEM((LANE, LANE), BF16)],
        compiler_params=_params(("parallel", "arbitrary")),
        name="moba_select",
    )(mobaf, qmoba)


def _softmax2(sp, sn, mn):
    sn = jnp.where(mn, sn, NEG)
    m = jnp.maximum(jnp.max(sp, axis=-1, keepdims=True), jnp.max(sn, axis=-1, keepdims=True))
    ep = jnp.exp(sp - m)
    en = jnp.where(mn, jnp.exp(sn - m), 0.0)
    den = jnp.sum(ep, axis=-1, keepdims=True) + jnp.sum(en, axis=-1, keepdims=True)
    return ep, en, den


def _new_mask(n_new_pad):
    qi = lax.broadcasted_iota(jnp.int32, (4 * N_HEADS, 1), 0) >> 3
    kj = lax.broadcasted_iota(jnp.int32, (1, n_new_pad), 1)
    return kj <= qi


def _mla_sample_kernel(pt_ref, qs_ref, new_ref, *rest, group, n_steps):
    pages, (o_ref, kt_ref) = rest[:group], rest[group:]
    b, s = pl.program_id(0), pl.program_id(1)

    @pl.when((b == 0) & (s == 0))
    def _():
        kt_ref[MLA_W:MLA_WP, :] = jnp.zeros((MLA_WP - MLA_W, kt_ref.shape[1]), BF16)

    for g in range(group):
        col = pl.multiple_of((s * group + g) * PAGE, PAGE)
        kt_ref[0:MLA_W, pl.ds(col, PAGE)] = pages[g][...].astype(BF16)

    @pl.when(s == n_steps - 1)
    def _():
        qs, new = qs_ref[...], new_ref[...]
        sp = _dot(qs, kt_ref[...]) * MLA_SCALE
        mn = _new_mask(new.shape[0])
        ep, en, den = _softmax2(sp, _dot_nt(qs, new) * MLA_SCALE, mn)
        o = _dot_nt(ep.astype(BF16), kt_ref[0:LANE, :]) + _dot(en.astype(BF16), new[:, 0:LANE])
        o_ref[...] = o / den


def _page_specs(layer, n_pages, group, feat):
    def spec(g):
        return pl.BlockSpec((None, None, feat, PAGE),
                            lambda b, s, pt: (layer, pt[b * n_pages + s * group + g], 0, 0))
    return [spec(g) for g in range(group)]


def _mla_sample(pt, qs, new, cache_t, layer, *, group=16):
    n_seq, n_pages = qs.shape[0], pt.shape[0] // qs.shape[0]
    group = min(group, n_pages)
    n_steps = n_pages // group
    per_seq = lambda a: pl.BlockSpec((None,) + a.shape[1:], lambda b, s, pt: (b, 0, 0))
    return pl.pallas_call(
        functools.partial(_mla_sample_kernel, group=group, n_steps=n_steps),
        grid_spec=pltpu.PrefetchScalarGridSpec(
            num_scalar_prefetch=1, grid=(n_seq, n_steps),
            in_specs=[per_seq(qs), per_seq(new)] + _page_specs(layer, n_pages, group, MLA_W),
            out_specs=pl.BlockSpec((None, 4 * N_HEADS, LANE), lambda b, s, pt: (b, 0, 0)),
            scratch_shapes=[pltpu.VMEM((MLA_WP, n_pages * PAGE), BF16)]),
        out_shape=jax.ShapeDtypeStruct((n_seq, 4 * N_HEADS, LANE), F32),
        compiler_params=_params(("arbitrary", "arbitrary")),
        name="mla_sample",
    )(pt, qs, new, *([cache_t] * group))


def _moba_sample_kernel(pt_ref, qs_ref, new_ref, shift_ref, expand_ref, *rest, group, n_steps, n_pages):
    pages, (o_ref, mt_ref, psum_ref, sc_ref) = rest[:group], rest[group:]
    s = pl.program_id(1)
    lane = lax.broadcasted_iota(jnp.int32, (1, LANE), 1)

    @pl.when(s == 0)
    def _():
        psum_ref[...] = jnp.zeros(psum_ref.shape, F32)

    for g in range(group):
        pg = s * group + g
        x = pages[g][...]
        mt_ref[:, pl.ds(pl.multiple_of(pg * PAGE, PAGE), PAGE)] = x.astype(BF16)
        psum_ref[...] += jnp.where(lane == pg, jnp.sum(x[0:LANE, :], axis=1, keepdims=True), 0.0)

    @pl.when(s == n_steps - 1)
    def _():
        qs, new = qs_ref[...], new_ref[...]
        ps = psum_ref[...]
        means = ((ps + pltpu.roll(ps, LANE - 1, 1)) * (1.0 / MOBA_BLOCK)).astype(BF16)
        gsc = _dot(qs, means)
        ok = ((lane & 1) == 0) & (lane < n_pages)
        sel = _topk_mask(jnp.where(ok, gsc, NEG), MOBA_TOPK, lane.astype(F32))
        sel = jnp.where(ok, sel, 0.0).astype(BF16)
        chunk = expand_ref.shape[1]
        for c in range(sc_ref.shape[1] // chunk):
            cs = slice(c * chunk, (c + 1) * chunk)
            picked = _dot(_dot(sel, shift_ref[c]).astype(BF16), expand_ref[...]) > 0.5
            sc_ref[:, cs] = jnp.where(picked, _dot(qs, mt_ref[0:LANE, cs]) * ATTN_SCALE, NEG)
        sp = sc_ref[...]
        mn = _new_mask(new.shape[0])
        ep, en, den = _softmax2(sp, _dot_nt(qs, new[:, 0:LANE]) * ATTN_SCALE, mn)
        ep = jnp.where(sp > 0.5 * NEG, ep, 0.0)
        o = _dot_nt(ep.astype(BF16), mt_ref[LANE:2 * LANE, :]) + _dot(en.astype(BF16), new[:, LANE:2 * LANE])
        o_ref[...] = o / den


def _moba_sample(pt, qs, new, cache_t, layer, *, group=16):
    n_seq, n_pages = qs.shape[0], pt.shape[0] // qs.shape[0]
    assert n_pages <= LANE and n_pages % 2 == 0
    group = min(group, n_pages)
    n_steps = n_pages // group
    chunk = min(2048, n_pages * PAGE)
    n_chunks = n_pages * PAGE // chunk
    bpc = chunk // MOBA_BLOCK
    lane_i = np.arange(LANE)
    shift = np.stack([(lane_i[:, None] == 2 * (c * bpc + lane_i[None, :])) & (lane_i[None, :] < bpc)
                      for c in range(n_chunks)]).astype(np.float32)
    expand = (lane_i[:, None] == (np.arange(chunk)[None, :] // MOBA_BLOCK)).astype(np.float32)
    shift, expand = jnp.asarray(shift, BF16), jnp.asarray(expand, BF16)
    per_seq = lambda a: pl.BlockSpec((None,) + a.shape[1:], lambda b, s, pt: (b, 0, 0))
    full = lambda a: pl.BlockSpec(a.shape, lambda b, s, pt: (0,) * a.ndim)
    return pl.pallas_call(
        functools.partial(_moba_sample_kernel, group=group, n_steps=n_steps, n_pages=n_pages),
        grid_spec=pltpu.PrefetchScalarGridSpec(
            num_scalar_prefetch=1, grid=(n_seq, n_steps),
            in_specs=[per_seq(qs), per_seq(new), full(shift), full(expand)]
            + _page_specs(layer, n_pages, group, 2 * LANE),
            out_specs=pl.BlockSpec((None, 4 * N_HEADS, LANE), lambda b, s, pt: (b, 0, 0)),
            scratch_shapes=[pltpu.VMEM((2 * LANE, n_pages * PAGE), BF16), pltpu.VMEM((LANE, LANE), F32),
                            pltpu.VMEM((4 * N_HEADS, n_pages * PAGE), F32)]),
        out_shape=jax.ShapeDtypeStruct((n_seq, 4 * N_HEADS, LANE), F32),
        compiler_params=_params(("arbitrary", "arbitrary")),
        name="moba_sample",
    )(pt, qs, new, shift, expand, *([cache_t] * group))


def _nsa_sample_kernel(pt_ref, qs_ref, newsel_ref, newwin_ref, win_ref, gate_ref, pat_ref, pbt_ref, perm_ref,
                       w1a_ref, w1b_ref, w2_ref, mslc_ref, shift_ref, expand_ref, *rest,
                       group, n_steps, n_cmp, past_len):
    cmp_pages, sel_pages = rest[:group], rest[group:2 * group]
    o_ref, zla_ref, zlb_ref, st_ref, sc_ref = rest[2 * group:]
    s = pl.program_id(1)

    for g in range(group):
        pg = s * group + g
        st_ref[:, pl.ds(pl.multiple_of(pg * PAGE, PAGE), PAGE)] = sel_pages[g][...].astype(BF16)
    perm = perm_ref[...]
    for g in range(0, group, 2):
        x = jnp.concatenate([cmp_pages[g][...], cmp_pages[g + 1][...]], axis=1)
        ra = _dot_nt(perm, (x + pat_ref[...]).astype(BF16)).astype(BF16)
        rb = _dot_nt(perm, (x + pbt_ref[...]).astype(BF16)).astype(BF16)
        r0 = pl.multiple_of((s * group + g) * 8, 16)
        for bb in range(CMP_STRIDE):
            zla_ref[pl.ds(r0, 16), bb * LANE:(bb + 1) * LANE] = ra[bb * 16:(bb + 1) * 16, :]
            zlb_ref[pl.ds(r0, 16), bb * LANE:(bb + 1) * LANE] = rb[bb * 16:(bb + 1) * 16, :]

    @pl.when(s == n_steps - 1)
    def _():
        qs = qs_ref[...]
        nrow = qs.shape[0]
        qidx = lax.broadcasted_iota(jnp.int32, (nrow // N_HEADS, 1, 1), 0)
        tok = _compress(zla_ref[...], zlb_ref[...], w1a_ref, w1b_ref, w2_ref).astype(BF16)
        o_cmp, imp = _cmp_attend(qs, tok, past_len + qidx, n_cmp, 1)
        imp = jnp.broadcast_to(imp[:, None, :], (nrow // N_HEADS, N_HEADS, imp.shape[1])).reshape(nrow, imp.shape[1])
        cur = (past_len + (lax.broadcasted_iota(jnp.int32, (nrow, 1), 0) >> 3)) >> 6
        sel = _select_blocks(imp, mslc_ref, cur)
        sel = sel.astype(BF16)
        chunk = expand_ref.shape[1]
        for c in range(sc_ref.shape[1] // chunk):
            cs = slice(c * chunk, (c + 1) * chunk)
            picked = _dot(_dot(sel, shift_ref[c]).astype(BF16), expand_ref[...]) > 0.5
            sc_ref[:, cs] = jnp.where(picked, _dot(qs, st_ref[:, cs]) * ATTN_SCALE, NEG)
        mn = _new_mask(newsel_ref.shape[0])
        sp = sc_ref[...]
        new = newsel_ref[...]
        ep, en, den = _softmax2(sp, _dot_nt(qs, new) * ATTN_SCALE, mn)
        ep = jnp.where(sp > 0.5 * NEG, ep, 0.0)
        o_sel = (_dot_nt(ep.astype(BF16), st_ref[...]) + _dot(en.astype(BF16), new)) / den
        wt = win_ref[...].astype(BF16)
        neww = newwin_ref[...]
        sw = _dot(qs, wt) * ATTN_SCALE
        vis = lax.broadcasted_iota(jnp.int32, (1, wt.shape[1]), 1) >= (lax.broadcasted_iota(jnp.int32, (nrow, 1), 0) >> 3)
        sw = jnp.where(vis, sw, NEG)
        ew, enw, denw = _softmax2(sw, _dot_nt(qs, neww) * ATTN_SCALE, mn)
        ew = jnp.where(vis, ew, 0.0)
        o_win = (_dot_nt(ew.astype(BF16), wt) + _dot(enw.astype(BF16), neww)) / denw
        gt = gate_ref[...]
        o_ref[...] = gt[:, 0:1] * o_cmp + gt[:, 1:2] * o_sel + gt[:, 2:3] * o_win


def _nsa_sample(pt, qs, newsel, newwin, win_t, gates, cmp_t, sel_t, wl, layer, *, group=16):
    n_seq, n_pages = qs.shape[0], pt.shape[0] // qs.shape[0]
    past_len = n_pages * PAGE
    group = min(group, n_pages)
    n_steps = n_pages // group
    nch = past_len // CMP_STRIDE
    length = past_len + 4
    n_cmp = (length - CMP_BLOCK) // CMP_STRIDE + 1
    n_sel = max(-(-length // SEL_BLOCK), SEL_TOPK)
    assert n_cmp + 1 == nch
    nsp = -(-n_sel // LANE) * LANE
    mslc = _slc_matrix(nch, n_cmp, nsp, n_sel)
    chunk = min(2048, past_len)
    n_chunks = past_len // chunk
    bpc = chunk // SEL_BLOCK
    shift = np.stack([(np.arange(nsp)[:, None] == c * bpc + np.arange(LANE)[None, :]) & (np.arange(LANE)[None, :] < bpc)
                      for c in range(n_chunks)]).astype(np.float32)
    expand = (np.arange(LANE)[:, None] == (np.arange(chunk)[None, :] // SEL_BLOCK)).astype(np.float32)
    r = np.arange(2 * PAGE)
    perm = (r[None, :] == (CMP_STRIDE * (r[:, None] % 16) + r[:, None] // 16)).astype(np.float32)
    shift, expand, perm = jnp.asarray(shift, BF16), jnp.asarray(expand, BF16), jnp.asarray(perm, BF16)
    consts = (wl['pat'], wl['pbt'], perm, wl['w1a'], wl['w1b'], wl['w2'], mslc, shift, expand)
    per_seq = lambda a: pl.BlockSpec((None,) + a.shape[1:], lambda b, s, pt: (b, 0, 0))
    full = lambda a: pl.BlockSpec(a.shape, lambda b, s, pt: (0,) * a.ndim)
    win_spec = pl.BlockSpec((None, None, LANE, win_t.shape[3]), lambda b, s, pt: (layer, b, 0, 0))
    return pl.pallas_call(
        functools.partial(_nsa_sample_kernel, group=group, n_steps=n_steps, n_cmp=n_cmp, past_len=past_len),
        grid_spec=pltpu.PrefetchScalarGridSpec(
            num_scalar_prefetch=1, grid=(n_seq, n_steps),
            in_specs=[per_seq(qs), per_seq(newsel), per_seq(newwin), win_spec, per_seq(gates)]
            + [full(a) for a in consts]
            + _page_specs(layer, n_pages, group, LANE) + _page_specs(layer, n_pages, group, LANE),
            out_specs=pl.BlockSpec((None, 4 * N_HEADS, LANE), lambda b, s, pt: (b, 0, 0)),
            scratch_shapes=[pltpu.VMEM((nch, CMP_STRIDE * LANE), BF16), pltpu.VMEM((nch, CMP_STRIDE * LANE), BF16),
                            pltpu.VMEM((LANE, past_len), BF16), pltpu.VMEM((4 * N_HEADS, past_len), F32)]),
        out_shape=jax.ShapeDtypeStruct((n_seq, 4 * N_HEADS, LANE), F32),
        compiler_params=_params(("arbitrary", "arbitrary")),
        name="nsa_sample",
    )(pt, qs, newsel, newwin, win_t, gates, *consts, *([cmp_t] * group), *([sel_t] * group))


def _merge_kernel(x_ref, g_ref, olat_ref, ocmp_ref, osel_ref, owin_ref, gate_ref, omoba_ref,
                  wgm_ref, wuv_ref, wb0_ref, wb1_ref, wb2_ref, wout_ref, o_ref):
    x = x_ref[...]
    d = x.shape[1]
    h = _rms(x, g_ref[...]).astype(BF16)
    gm = jax.nn.sigmoid(_dot(h, wgm_ref[...]))
    up0 = _dot(_dot(olat_ref[...].astype(BF16), wuv_ref[...]).astype(BF16), wb0_ref[...])
    gt = gate_ref[...]
    parts = []
    for hh in range(N_HEADS):
        blk = slice(hh * LANE, (hh + 1) * LANE)
        parts.append(gt[:, hh:hh + 1] * ocmp_ref[:, blk] + gt[:, N_HEADS + hh:N_HEADS + hh + 1] * osel_ref[:, blk]
                     + gt[:, 2 * N_HEADS + hh:2 * N_HEADS + hh + 1] * owin_ref[:, blk])
    up1 = _dot(jnp.concatenate(parts, axis=1).astype(BF16), wb1_ref[...])
    up2 = _dot(omoba_ref[...].astype(BF16), wb2_ref[...])
    mix = gm[:, 0:d] * up0 + gm[:, d:2 * d] * up1 + gm[:, 2 * d:3 * d] * up2
    o_ref[...] = x + _dot(mix.astype(BF16), wout_ref[...])


def _merge(x, g, olat, ocmp, osel, owin, gates, omoba, wl, tm):
    t, d = x.shape
    row = lambda a: pl.BlockSpec((tm, a.shape[1]), lambda i: (i, 0))
    full = lambda a: pl.BlockSpec(a.shape, lambda i: (0,) * a.ndim)
    acts = (olat, ocmp, osel, owin, gates, omoba)
    consts = (wl['wgm'], wl['wuv'], wl['wb0'], wl['wb1'], wl['wb2'], wl['wout'])
    return pl.pallas_call(
        _merge_kernel,
        grid=(t // tm,),
        in_specs=[row(x), full(g)] + [row(a) for a in acts] + [full(a) for a in consts],
        out_specs=row(x),
        out_shape=jax.ShapeDtypeStruct((t, d), F32),
        compiler_params=_params(("parallel",)),
        name="merge",
    )(x, g, *acts, *consts)


def _ffn_kernel(x_ref, g_ref, gf_ref, wg_ref, wu_ref, wd_ref, o_ref, h_ref, acc_ref, *, nf, final):
    f = pl.program_id(1)

    @pl.when(f == 0)
    def _():
        h_ref[...] = _rms(x_ref[...], g_ref[...]).astype(BF16)
        acc_ref[...] = jnp.zeros(acc_ref.shape, F32)

    h = h_ref[...]
    act = jax.nn.silu(_dot(h, wg_ref[...])) * _dot(h, wu_ref[...])
    acc_ref[...] += _dot(act.astype(BF16), wd_ref[...])

    @pl.when(f == nf - 1)
    def _():
        y = x_ref[...] + acc_ref[...]
        o_ref[...] = _rms(y, gf_ref[...]) if final else y


def _ffn(x, g, gfinal, wgu, wd, *, final, tm=512, tf=512):
    t, d = x.shape
    tm = tm if t % tm == 0 else 256
    dff = wd.shape[0]
    nf = dff // tf
    return pl.pallas_call(
        functools.partial(_ffn_kernel, nf=nf, final=final),
        grid=(t // tm, nf),
        in_specs=[pl.BlockSpec((tm, d), lambda i, f: (i, 0)), pl.BlockSpec((1, d), lambda i, f: (0, 0)),
                  pl.BlockSpec((1, d), lambda i, f: (0, 0)),
                  pl.BlockSpec((d, tf), lambda i, f: (0, f)), pl.BlockSpec((d, tf), lambda i, f: (0, nf + f)),
                  pl.BlockSpec((tf, d), lambda i, f: (f, 0))],
        out_specs=pl.BlockSpec((tm, d), lambda i, f: (i, 0)),
        out_shape=jax.ShapeDtypeStruct((t, d), F32),
        scratch_shapes=[pltpu.VMEM((tm, d), BF16), pltpu.VMEM((tm, d), F32)],
        compiler_params=_params(("parallel", "arbitrary")),
        name="ffn",
    )(x, g, gfinal, wgu, wgu, wd)


def _moe_kernel(x_ref, g_ref, gf_ref, wr_ref, wg_ref, wu_ref, wd_ref, o_ref, h_ref, gate_ref, acc_ref,
                *, ne, nf, final):
    e, f = pl.program_id(1), pl.program_id(2)
    lane = lax.broadcasted_iota(jnp.int32, (1, LANE), 1)

    @pl.when((e == 0) & (f == 0))
    def _():
        h = _rms(x_ref[...], g_ref[...]).astype(BF16)
        h_ref[...] = h
        acc_ref[...] = jnp.zeros(acc_ref.shape, F32)
        lanef = lane.astype(F32)
        logits = jnp.where(lane < ne, _dot(h, wr_ref[...]), NEG)
        v1 = jnp.max(logits, axis=-1, keepdims=True)
        i1 = jnp.min(jnp.where(logits == v1, lanef, 1e9), axis=-1, keepdims=True)
        rest = jnp.where(lanef == i1, NEG, logits)
        v2 = jnp.max(rest, axis=-1, keepdims=True)
        i2 = jnp.min(jnp.where(rest == v2, lanef, 1e9), axis=-1, keepdims=True)
        e2 = jnp.exp(v2 - v1)
        gate_ref[...] = jnp.where(lanef == i1, 1.0 / (1.0 + e2), 0.0) + jnp.where(lanef == i2, e2 / (1.0 + e2), 0.0)

    h = h_ref[...]
    act = jax.nn.silu(_dot(h, wg_ref[...])) * _dot(h, wu_ref[...])
    ge = jnp.sum(jnp.where(lane == e, gate_ref[...], 0.0), axis=-1, keepdims=True)
    acc_ref[...] += ge * _dot(act.astype(BF16), wd_ref[...])

    @pl.when((e == ne - 1) & (f == nf - 1))
    def _():
        y = x_ref[...] + acc_ref[...]
        o_ref[...] = _rms(y, gf_ref[...]) if final else y


def _moe(x, g, gfinal, router, wgu, wd, *, final, tm=512, tf=512):
    t, d = x.shape
    tm = tm if t % tm == 0 else 256
    ne, dff = wd.shape[0], wd.shape[1]
    nf = dff // tf
    return pl.pallas_call(
        functools.partial(_moe_kernel, ne=ne, nf=nf, final=final),
        grid=(t // tm, ne, nf),
        in_specs=[pl.BlockSpec((tm, d), lambda i, e, f: (i, 0)), pl.BlockSpec((1, d), lambda i, e, f: (0, 0)),
                  pl.BlockSpec((1, d), lambda i, e, f: (0, 0)), pl.BlockSpec((d, LANE), lambda i, e, f: (0, 0)),
                  pl.BlockSpec((None, d, tf), lambda i, e, f: (e, 0, f)),
                  pl.BlockSpec((None, d, tf), lambda i, e, f: (e, 0, nf + f)),
                  pl.BlockSpec((None, tf, d), lambda i, e, f: (e, f, 0))],
        out_specs=pl.BlockSpec((tm, d), lambda i, e, f: (i, 0)),
        out_shape=jax.ShapeDtypeStruct((t, d), F32),
        scratch_shapes=[pltpu.VMEM((tm, d), BF16), pltpu.VMEM((tm, LANE), F32), pltpu.VMEM((tm, d), F32)],
        compiler_params=_params(("parallel", "arbitrary", "arbitrary")),
        name="moe",
    )(x, g, gfinal, router, wgu, wgu, wd)


def _rope_tables(pos):
    posf = pos.astype(F32)[:, None]

    def cs(rot, theta):
        inv = 1.0 / (theta ** (jnp.arange(0, rot, 2, dtype=F32) / rot))
        ang = posf * inv[None, :]
        return jnp.cos(ang), jnp.sin(ang)

    t = pos.shape[0]
    cp, sp = cs(2 * ROT_HALF, ROPE_THETA)
    one, zero = jnp.ones((t, HEAD_DIM - 2 * ROT_HALF), F32), jnp.zeros((t, HEAD_DIM - 2 * ROT_HALF), F32)
    z8 = jnp.zeros((t, ROT_HALF), F32)
    cd = jnp.tile(jnp.concatenate([cp, cp, one], 1), (1, 2))
    s1d = jnp.tile(jnp.concatenate([-sp, z8, zero], 1), (1, 2))
    s2d = jnp.tile(jnp.concatenate([z8, sp, zero], 1), (1, 2))
    cmm, smm = cs(MLA_ROPE, MLA_THETA)
    hm = MLA_ROPE // 2
    zm, onem = jnp.zeros((t, hm), F32), jnp.ones((t, LANE - MLA_ROPE), F32)
    zrest = jnp.zeros((t, LANE - MLA_ROPE), F32)
    cm = jnp.concatenate([cmm, cmm, onem], 1)
    s1m = jnp.concatenate([-smm, zm, zrest], 1)
    s2m = jnp.concatenate([zm, smm, zrest], 1)
    return cd, s1d, s2d, cm, s1m, s2m


def _pad_heads(w, offsets):
    out = jnp.zeros((w.shape[0], N_HEADS, LANE), w.dtype)
    for hh in range(N_HEADS):
        out = out.at[:, hh, offsets[hh]:offsets[hh] + HEAD_DIM].set(w[:, hh])
    return out.reshape(w.shape[0], N_HEADS * LANE)


def _layer_weights(l, w_in, mla_q_norm, mla_kv_norm, mla_w_uq, mla_w_uk, mla_w_uv, nsa_cmp_pos, nsa_cmp_w1,
                   nsa_cmp_w2, w_branch, w_out):
    d = w_in.shape[1]
    w = w_in[l]
    sizes = (MLA_Q_LORA, MLA_KV_LORA, MLA_ROPE, N_HEADS * HEAD_DIM, 2 * HEAD_DIM, 2 * HEAD_DIM, 2 * HEAD_DIM,
             3 * N_HEADS, N_HEADS * HEAD_DIM, 2 * MOBA_KVH * HEAD_DIM, 3 * d)
    pts = [int(v) for v in np.cumsum(sizes)[:-1]]
    cq, ckv, kr, qb, kvc, kvs, kvw, gb, qc, kvm, gm = jnp.split(w, pts, axis=1)
    padl = lambda a: jnp.pad(a, ((0, 0), (0, LANE - a.shape[1])))
    moba_off = [HEAD_DIM * (hh // (N_HEADS // MOBA_KVH)) for hh in range(N_HEADS)]
    wp = jnp.concatenate([cq, ckv, padl(kr), _pad_heads(qb.reshape(d, N_HEADS, HEAD_DIM), [0] * N_HEADS), kvc, kvs,
                          kvw, padl(gb), _pad_heads(qc.reshape(d, N_HEADS, HEAD_DIM), moba_off), kvm], axis=1)
    assert wp.shape[1] == C_END
    eye_h, eye_2 = jnp.eye(N_HEADS, dtype=F32), jnp.eye(2, dtype=F32)
    uq = mla_w_uq[l]
    wuqr = jnp.pad(uq[:, :, MLA_NOPE:], ((0, 0), (0, 0), (0, LANE - MLA_ROPE))).reshape(MLA_Q_LORA, N_HEADS * LANE)
    w1 = nsa_cmp_w1[l]
    half = lambda a: jnp.einsum('cbde,cg->bcdge', a, eye_2).reshape(CMP_STRIDE * LANE, 2 * CMP_HIDDEN)
    pos = nsa_cmp_pos[l].reshape(CMP_BLOCK, LANE)
    pa, pb = pos[:CMP_STRIDE], pos[CMP_STRIDE:]
    wb = w_branch[l]
    wb1 = jnp.pad(wb[1].reshape(N_HEADS, HEAD_DIM, d), ((0, 0), (HEAD_DIM, 0), (0, 0))).reshape(N_HEADS * LANE, d)
    wb2 = _pad_heads(wb[2].reshape(N_HEADS, HEAD_DIM, d).transpose(2, 0, 1), moba_off).T
    bf = lambda a: a.astype(BF16)
    return dict(
        wp=bf(wp), wgm=bf(gm), qn=mla_q_norm[l][None], kvn=mla_kv_norm[l][None],
        wuqn=bf(uq[:, :, :MLA_NOPE].reshape(MLA_Q_LORA, N_HEADS * MLA_NOPE)), wuqr=bf(wuqr),
        wuk=bf(jnp.einsum('chd,hg->hdgc', mla_w_uk[l], eye_h).reshape(N_HEADS * MLA_NOPE, N_HEADS * MLA_KV_LORA)),
        wuv=bf(jnp.einsum('chv,hg->hcgv', mla_w_uv[l], eye_h).reshape(N_HEADS * MLA_KV_LORA, N_HEADS * HEAD_DIM)),
        w1a=bf(half(w1[:, :CMP_STRIDE])), w1b=bf(half(w1[:, CMP_STRIDE:])),
        w2=bf(jnp.einsum('ced,cg->cegd', nsa_cmp_w2[l], eye_2).reshape(2 * CMP_HIDDEN, LANE)),
        pa=pa.reshape(1, CMP_STRIDE * LANE), pb=pb.reshape(1, CMP_STRIDE * LANE),
        pat=jnp.tile(pa.T, (1, 2 * PAGE // CMP_STRIDE)), pbt=jnp.tile(pb.T, (1, 2 * PAGE // CMP_STRIDE)),
        wb0=bf(wb[0]), wb1=bf(wb1), wb2=bf(wb2), wout=bf(w_out[l]))


def kernel(x_prompt, x_sample, cache_mla, cache_nsa_cmp, cache_nsa_sel, state_nsa_win, cache_moba, page_table, norm_mix, w_in, mla_q_norm, mla_kv_norm, mla_w_uq, mla_w_uk, mla_w_uv, nsa_cmp_pos, nsa_cmp_w1, nsa_cmp_w2, w_branch, w_out, norm_ffn, ffn_w_gate_up, ffn_w_down, moe_router, moe_w_gate_up, moe_w_down, norm_final):
    nb, seq, d = x_prompt.shape
    ns, nn, _ = x_sample.shape
    depth = w_in.shape[0]
    n_pages = page_table.shape[1]
    past_len = n_pages * PAGE
    tp, ts = nb * seq, ns * nn
    assert nn == 4 and ts % 256 == 0 and seq % 512 == 0

    x = jnp.concatenate([x_prompt.reshape(tp, d), x_sample.reshape(ts, d)], axis=0)
    pos = jnp.concatenate([jnp.tile(jnp.arange(seq), nb), jnp.tile(past_len + jnp.arange(nn), ns)])
    tabs = _rope_tables(pos)
    pt = page_table.reshape(-1)
    npool = cache_mla.shape[1]
    mla_t = jnp.transpose(cache_mla, (0, 1, 3, 2))
    cmp_t = jnp.transpose(cache_nsa_cmp, (0, 1, 3, 4, 2)).reshape(depth, npool, LANE, PAGE)
    sel_t = jnp.transpose(cache_nsa_sel, (0, 1, 3, 4, 2)).reshape(depth, npool, LANE, PAGE)
    moba_t = jnp.transpose(cache_moba, (0, 1, 3, 4, 5, 2)).reshape(depth, npool, 2 * LANE, PAGE)
    win_t = jnp.transpose(state_nsa_win, (0, 1, 3, 4, 2)).reshape(depth, ns, LANE, state_nsa_win.shape[2])

    def stack_q(a):
        return a[tp:].reshape(ns, nn * N_HEADS, a.shape[1] // N_HEADS)

    def new_rows(a):
        return jnp.pad(a[tp:].reshape(ns, nn, a.shape[1]), ((0, 0), (0, 16 - nn), (0, 0)))

    outs = [[] for _ in range(10)]
    for l in range(depth):
        wl = _layer_weights(l, w_in, mla_q_norm, mla_kv_norm, mla_w_uq, mla_w_uk, mla_w_uv, nsa_cmp_pos,
                            nsa_cmp_w1, nsa_cmp_w2, w_branch, w_out)
        g_mix = norm_mix[l][None]
        (qmla, mlaf, mlab, qnsa, cmpf, self_, selb, winf, winb, gates, qmoba, mobaf, mobab) = _project(
            x, g_mix, wl, tabs, 256)

        olat_p = _prompt_attention(qmla, mlab, None, mode='mla', n_seq=nb, seq=seq)
        ocmp_p, selmask = _nsa_select_prompt(cmpf, qnsa, wl, n_seq=nb, seq=seq)
        osel_p = _prompt_attention(qnsa, selb, selmask, mode='sel', n_seq=nb, seq=seq)
        owin_p = _prompt_attention(qnsa, winb, None, mode='win', n_seq=nb, seq=seq)
        mobasel = _moba_select_prompt(mobaf, qmoba, n_seq=nb, seq=seq)
        omoba_p = _prompt_attention(qmoba, mobab, mobasel, mode='moba', n_seq=nb, seq=seq)

        olat_s = _mla_sample(pt, stack_q(qmla), new_rows(mlab), mla_t, l)
        gs = gates[tp:, 0:3 * N_HEADS].reshape(ns, nn, 3, N_HEADS).transpose(0, 1, 3, 2).reshape(ns, nn * N_HEADS, 3)
        gs = jnp.pad(gs, ((0, 0), (0, 0), (0, LANE - 3)))
        onsa_s = _nsa_sample(pt, stack_q(qnsa), new_rows(selb), new_rows(winb), win_t, gs, cmp_t, sel_t, wl, l)
        omoba_s = _moba_sample(pt, stack_q(qmoba), new_rows(mobab), moba_t, l)

        unstack = lambda o: o.reshape(ts, N_HEADS * LANE)
        olat = jnp.concatenate([olat_p, unstack(olat_s)], axis=0)
        ocmp = jnp.concatenate([ocmp_p, unstack(onsa_s)], axis=0)
        zeros_s = jnp.zeros((ts, N_HEADS * LANE), F32)
        osel = jnp.concatenate([osel_p, zeros_s], axis=0)
        owin = jnp.concatenate([owin_p, zeros_s], axis=0)
        unit = jnp.concatenate([jnp.ones((ts, N_HEADS), F32), jnp.zeros((ts, LANE - N_HEADS), F32)], axis=1)
        gates_m = jnp.concatenate([gates[:tp], unit], axis=0)
        omoba = jnp.concatenate([omoba_p, unstack(omoba_s)], axis=0)

        x = _merge(x, g_mix, olat, ocmp, osel, owin, gates_m, omoba, wl, 256)
        last = l == depth - 1
        g_ffn, g_fin = norm_ffn[l][None], norm_final[None]
        if l % 2 == 0:
            x = _ffn(x, g_ffn, g_fin, ffn_w_gate_up[l // 2].astype(BF16), ffn_w_down[l // 2].astype(BF16), final=last)
        else:
            router = jnp.pad(moe_router[l // 2], ((0, 0), (0, LANE - N_EXPERTS))).astype(BF16)
            x = _moe(x, g_ffn, g_fin, router, moe_w_gate_up[l // 2].astype(BF16), moe_w_down[l // 2].astype(BF16),
                     final=last)

        win_prompt = winf[:tp].reshape(nb, seq, 2, HEAD_DIM)[:, seq - min(NSA_WINDOW, seq):]
        win_new = winf[tp:].reshape(ns, nn, 2, HEAD_DIM)
        rows = (mlaf[:tp].reshape(nb, seq, MLA_W), mlaf[tp:].reshape(ns, nn, MLA_W),
                cmpf[:tp].reshape(nb, seq, 2, HEAD_DIM), cmpf[tp:].reshape(ns, nn, 2, HEAD_DIM),
                self_[:tp].reshape(nb, seq, 2, HEAD_DIM), self_[tp:].reshape(ns, nn, 2, HEAD_DIM),
                win_prompt, jnp.concatenate([state_nsa_win[l], win_new], axis=1)[:, nn:],
                mobaf[:tp].reshape(nb, seq, 2, MOBA_KVH, HEAD_DIM), mobaf[tp:].reshape(ns, nn, 2, MOBA_KVH, HEAD_DIM))
        for acc, r in zip(outs, rows):
            acc.append(r)

    y_prompt = x[:tp].reshape(nb, seq, d)
    y_sample = x[tp:].reshape(ns, nn, d)
    return (y_prompt, y_sample) + tuple(jnp.stack(o) for o in outs)
```

```python
import functools

import numpy as np
import jax
import jax.numpy as jnp
from jax import lax
from jax.experimental import pallas as pl
from jax.experimental.pallas import tpu as pltpu

F32 = jnp.float32
BF16 = jnp.bfloat16
NEG = -1e30
MASK_FLOOR = -1e29
RMS_EPS = 1e-6

N_HEADS = 8
HEAD_DIM = 64
LANE = 128
ROT_HALF = 8
ROPE_THETA = 500000.0
ATTN_SCALE = HEAD_DIM ** -0.5
MLA_Q_LORA = 256
MLA_KV_LORA = 128
MLA_NOPE = 64
MLA_ROPE = 32
MLA_THETA = 10000.0
MLA_SCALE = (MLA_NOPE + MLA_ROPE) ** -0.5
MLA_W = MLA_KV_LORA + MLA_ROPE
MLA_WP = 256
CMP_BLOCK = 32
CMP_STRIDE = 16
CMP_HIDDEN = 256
SEL_BLOCK = 64
SEL_TOPK = 16
NSA_WINDOW = 512
NSA_FORCE = 1000.0
MOBA_BLOCK = 256
MOBA_TOPK = 3
MOBA_KVH = 2
N_EXPERTS = 8
PAGE = 128

VMEM_LIMIT = 56 * 1024 * 1024

C_CQ, C_CKV, C_KR, C_QB, C_CMP, C_SEL, C_WIN, C_GB, C_QC, C_KVM, C_END = (
    0, 256, 384, 512, 1536, 1664, 1792, 1920, 2048, 3072, 3328)


def _dot(a, b):
    return jnp.dot(a, b, preferred_element_type=F32)


def _dot_nt(a, b):
    return lax.dot_general(a, b, (((1,), (1,)), ((), ())), preferred_element_type=F32)


def _rms(x, g):
    return x * lax.rsqrt(jnp.mean(x * x, axis=-1, keepdims=True) + RMS_EPS) * g


def _rope(v, c, s1, s2, half):
    return v * c + pltpu.roll(v, LANE - half, 1) * s1 + pltpu.roll(v, half, 1) * s2


def _topk_mask(score, k, jbf):
    sel = jnp.zeros(score.shape, F32)
    sc = score
    for _ in range(k):
        mx = jnp.max(sc, axis=-1, keepdims=True)
        idx = jnp.min(jnp.where(sc == mx, jbf, 1e9), axis=-1, keepdims=True)
        pick = jbf == idx
        sel = jnp.where(pick, 1.0, sel)
        sc = jnp.where(pick, -jnp.inf, sc)
    return sel


def _params(sem):
    return pltpu.CompilerParams(dimension_semantics=sem, vmem_limit_bytes=VMEM_LIMIT)


def _proj_kernel(x_ref, g_ref, wp_ref, qn_ref, kvn_ref, wuqn_ref, wuqr_ref, wuk_ref,
                 cd_ref, s1d_ref, s2d_ref, cm_ref, s1m_ref, s2m_ref,
                 qmla_ref, mlaf_ref, mlab_ref, qnsa_ref, cmpf_ref, self_ref, selb_ref, winf_ref, winb_ref,
                 gate_ref, qmoba_ref, mobaf_ref, mobab_ref):
    h = _rms(x_ref[...], g_ref[...]).astype(BF16)

    def seg(a, b):
        return _dot(h, wp_ref[:, a:b])

    lo = lax.broadcasted_iota(jnp.int32, (1, LANE), 1) < HEAD_DIM
    cd, s1d, s2d = cd_ref[...], s1d_ref[...], s2d_ref[...]
    cs, s1s, s2s = jnp.where(lo, cd, 1.0), jnp.where(lo, s1d, 0.0), jnp.where(lo, s2d, 0.0)
    cm, s1m, s2m = cm_ref[...], s1m_ref[...], s2m_ref[...]

    cqn = _rms(seg(C_CQ, C_CKV), qn_ref[...]).astype(BF16)
    qlat = _dot(_dot(cqn, wuqn_ref[...]).astype(BF16), wuk_ref[...])
    qr = _dot(cqn, wuqr_ref[...])
    for hh in range(N_HEADS):
        blk = slice(hh * LANE, (hh + 1) * LANE)
        qmla_ref[:, hh * MLA_WP:hh * MLA_WP + LANE] = qlat[:, blk].astype(BF16)
        qmla_ref[:, hh * MLA_WP + LANE:(hh + 1) * MLA_WP] = _rope(qr[:, blk], cm, s1m, s2m, MLA_ROPE // 2).astype(BF16)

    ckvn = _rms(seg(C_CKV, C_KR), kvn_ref[...])
    kr = _rope(seg(C_KR, C_QB), cm, s1m, s2m, MLA_ROPE // 2)
    mlaf_ref[:, 0:MLA_KV_LORA] = ckvn
    mlaf_ref[:, MLA_KV_LORA:MLA_W] = kr[:, 0:MLA_ROPE]
    mlab_ref[:, 0:LANE] = ckvn.astype(BF16)
    mlab_ref[:, LANE:MLA_WP] = kr.astype(BF16)

    qb = seg(C_QB, C_CMP)
    for hh in range(N_HEADS):
        blk = slice(hh * LANE, (hh + 1) * LANE)
        qnsa_ref[:, blk] = _rope(qb[:, blk], cd, s1d, s2d, ROT_HALF).astype(BF16)
    cmpf_ref[...] = _rope(seg(C_CMP, C_SEL), cs, s1s, s2s, ROT_HALF)
    sl = _rope(seg(C_SEL, C_WIN), cs, s1s, s2s, ROT_HALF)
    self_ref[...] = sl
    selb_ref[...] = sl.astype(BF16)
    wn = _rope(seg(C_WIN, C_GB), cs, s1s, s2s, ROT_HALF)
    winf_ref[...] = wn
    winb_ref[...] = wn.astype(BF16)
    gate_ref[...] = jax.nn.sigmoid(seg(C_GB, C_QC))

    qc = seg(C_QC, C_KVM)
    for hh in range(N_HEADS):
        blk = slice(hh * LANE, (hh + 1) * LANE)
        qmoba_ref[:, blk] = _rope(qc[:, blk], cd, s1d, s2d, ROT_HALF).astype(BF16)
    kvm = seg(C_KVM, C_END)
    km = _rope(kvm[:, 0:LANE], cd, s1d, s2d, ROT_HALF)
    mobaf_ref[:, 0:LANE] = km
    mobaf_ref[:, LANE:2 * LANE] = kvm[:, LANE:2 * LANE]
    mobab_ref[:, 0:LANE] = km.astype(BF16)
    mobab_ref[:, LANE:2 * LANE] = kvm[:, LANE:2 * LANE].astype(BF16)


def _project(x, g, wl, tabs, tm):
    t, d = x.shape
    row = lambda w: pl.BlockSpec((tm, w), lambda i: (i, 0))
    full = lambda a: pl.BlockSpec(a.shape, lambda i: (0,) * a.ndim)
    consts = (g, wl['wp'], wl['qn'], wl['kvn'], wl['wuqn'], wl['wuqr'], wl['wuk'])
    outs = [(N_HEADS * MLA_WP, BF16), (MLA_W, F32), (MLA_WP, BF16), (N_HEADS * LANE, BF16), (LANE, F32),
            (LANE, F32), (LANE, BF16), (LANE, F32), (LANE, BF16), (LANE, F32), (N_HEADS * LANE, BF16),
            (2 * LANE, F32), (2 * LANE, BF16)]
    return pl.pallas_call(
        _proj_kernel,
        grid=(t // tm,),
        in_specs=[row(d)] + [full(a) for a in consts] + [row(LANE)] * 6,
        out_specs=[row(w) for w, _ in outs],
        out_shape=[jax.ShapeDtypeStruct((t, w), dt) for w, dt in outs],
        compiler_params=_params(("parallel",)),
        name="proj",
    )(x, *consts, *tabs)


def _attn_schedule(seq, tq, tk, mode):
    steps = []
    for i in range(seq // tq):
        hi = (i * tq + tq - 1) // tk
        lo = max(i * tq - NSA_WINDOW, 0) // tk if mode == 'win' else 0
        steps += [(i, kb, int(kb == lo), int(kb == hi)) for kb in range(lo, hi + 1)]
    return np.asarray(steps, np.int32).T


def _attn_kernel(sched_ref, *refs, mode, tq, tk, kw, voff, scale, hg):
    if mode in ('sel', 'moba'):
        q_ref, kv_ref, sel_ref, o_ref, qs_ref, m_ref, l_ref, acc_ref, sels_ref = refs
    else:
        q_ref, kv_ref, o_ref, qs_ref, m_ref, l_ref, acc_ref = refs
    t = pl.program_id(1)
    i, kb = sched_ref[0, t], sched_ref[1, t]

    @pl.when(sched_ref[2, t] == 1)
    def _():
        for hh in range(N_HEADS):
            qs_ref[hh * tq:(hh + 1) * tq, :] = q_ref[:, hh * kw:(hh + 1) * kw]
            if mode == 'moba':
                sels_ref[hh * tq:(hh + 1) * tq, :] = sel_ref[:, hh * LANE:(hh + 1) * LANE]
        m_ref[...] = jnp.full(m_ref.shape, MASK_FLOOR, F32)
        l_ref[...] = jnp.zeros(l_ref.shape, F32)
        acc_ref[...] = jnp.zeros(acc_ref.shape, F32)

    kpos = kb * tk + lax.broadcasted_iota(jnp.int32, (1, tk), 1)
    qpos = i * tq + lax.broadcasted_iota(jnp.int32, (tq, 1), 0)
    causal = kpos <= qpos
    if mode == 'mla':
        mask2 = causal
    elif mode == 'win':
        mask2 = causal & (qpos - kpos <= NSA_WINDOW)
    else:
        shift = 6 if mode == 'sel' else 8
        expand = jnp.where((kpos >> shift) == lax.broadcasted_iota(jnp.int32, (LANE, 1), 0), 1.0, 0.0).astype(BF16)
        if mode == 'sel':
            mask2 = (_dot(sel_ref[...], expand) > 0.5) & causal
        else:
            mask2 = causal & ((kpos >> shift) == (qpos >> shift))
    bias = jnp.where(mask2, 1.0, 0.0) if mode == 'moba' else jnp.where(mask2, 0.0, NEG)
    for g in range(N_HEADS // hg):
        hs = slice(g * hg, (g + 1) * hg)
        rs = slice(g * hg * tq, (g + 1) * hg * tq)
        s3 = (_dot_nt(qs_ref[rs, :], kv_ref[:, 0:kw]) * scale).reshape(hg, tq, tk)
        if mode == 'moba':
            vis = jnp.maximum(_dot(sels_ref[rs, :], expand).reshape(hg, tq, tk), bias[None])
            s3 = s3 + (vis - 1.0) * (-NEG)
        else:
            s3 = s3 + bias[None]
        m_old = m_ref[hs]
        m_new = jnp.maximum(m_old, jnp.max(s3, axis=-1, keepdims=True))
        p3 = jnp.exp(s3 - m_new)
        alpha = jnp.exp(m_old - m_new)
        l_ref[hs] = alpha * l_ref[hs] + jnp.sum(p3, axis=-1, keepdims=True)
        pv = _dot(p3.reshape(hg * tq, tk).astype(BF16), kv_ref[:, voff:voff + LANE])
        acc_ref[hs] = alpha * acc_ref[hs] + pv.reshape(hg, tq, LANE)
        m_ref[hs] = m_new

    @pl.when(sched_ref[3, t] == 1)
    def _():
        o = acc_ref[...] / jnp.maximum(l_ref[...], 1e-30)
        for hh in range(N_HEADS):
            o_ref[:, hh * LANE:(hh + 1) * LANE] = o[hh]


def _prompt_attention(q, kv, sel, *, mode, n_seq, seq, tq=128, tk=512, hg=2):
    tk = min(tk, seq)
    kw = q.shape[1] // N_HEADS
    wkv = kv.shape[1]
    voff = LANE if mode == 'moba' else 0
    scale = MLA_SCALE if mode == 'mla' else ATTN_SCALE
    nq, nkb = seq // tq, seq // tk
    sched = _attn_schedule(seq, tq, tk, mode)
    q_map = lambda b, t, sc: (b * nq + sc[0, t], 0)
    kv_map = lambda b, t, sc: (b * nkb + sc[1, t], 0)
    in_specs = [pl.BlockSpec((tq, N_HEADS * kw), q_map), pl.BlockSpec((tk, wkv), kv_map)]
    args = [q, kv]
    scratch = [pltpu.VMEM((N_HEADS * tq, kw), BF16), pltpu.VMEM((N_HEADS, tq, 1), F32),
               pltpu.VMEM((N_HEADS, tq, 1), F32), pltpu.VMEM((N_HEADS, tq, LANE), F32)]
    if mode in ('sel', 'moba'):
        in_specs.append(pl.BlockSpec((tq, sel.shape[1]), q_map))
        args.append(sel)
        scratch.append(pltpu.VMEM((N_HEADS * tq, LANE), BF16))
    return pl.pallas_call(
        functools.partial(_attn_kernel, mode=mode, tq=tq, tk=tk, kw=kw, voff=voff, scale=scale, hg=hg),
        grid_spec=pltpu.PrefetchScalarGridSpec(
            num_scalar_prefetch=1, grid=(n_seq, sched.shape[1]), in_specs=in_specs,
            out_specs=pl.BlockSpec((tq, N_HEADS * LANE), q_map), scratch_shapes=scratch),
        out_shape=jax.ShapeDtypeStruct((n_seq * seq, N_HEADS * LANE), F32),
        compiler_params=_params(("parallel", "arbitrary")),
        name="attn_" + mode,
    )(jnp.asarray(sched), *args)


def _compress(zla, zlb, w1a_ref, w1b_ref, w2_ref):
    a = _dot(zla, w1a_ref[...])
    b = _dot(zlb, w1b_ref[...])
    hid = jax.nn.gelu(a + pltpu.roll(b, b.shape[0] - 1, 0))
    return _dot(hid.astype(BF16), w2_ref[...])


def _cmp_attend(qs, tok, qpos, n_cmp, head_axis):
    ncp = tok.shape[0]
    rows = qs.shape[0]
    nq = rows // N_HEADS
    shape3 = (N_HEADS, nq, ncp) if head_axis == 0 else (nq, N_HEADS, ncp)
    s3 = (_dot_nt(qs, tok) * ATTN_SCALE).reshape(shape3)
    nidx = lax.broadcasted_iota(jnp.int32, (1, 1, ncp), 2)
    mask = ((nidx * CMP_STRIDE + (CMP_BLOCK - 1)) <= qpos) & (nidx < n_cmp)
    s3 = jnp.where(mask, s3, NEG)
    e = jnp.where(mask, jnp.exp(s3 - jnp.max(s3, axis=-1, keepdims=True)), 0.0)
    p3 = e / jnp.maximum(jnp.sum(e, axis=-1, keepdims=True), 1e-30)
    o = _dot(p3.reshape(rows, ncp).astype(BF16), tok)
    return o, jnp.sum(p3, axis=head_axis)


def _select_blocks(imp, mslc_ref, cur):
    pslc = jnp.dot(imp, mslc_ref[...], precision=lax.Precision.HIGHEST, preferred_element_type=F32)
    jb = lax.broadcasted_iota(jnp.int32, (1, pslc.shape[1]), 1)
    forced = (jb == 0) | (jb == cur) | (jb == cur - 1)
    score = jnp.where(jb <= cur, pslc + NSA_FORCE * forced.astype(F32), NEG)
    sel = _topk_mask(score, SEL_TOPK, jb.astype(F32))
    return jnp.where(jb <= cur, sel, 0.0)


def _nsa_select_kernel(cmp_ref, q_ref, pa_ref, pb_ref, w1a_ref, w1b_ref, w2_ref, mslc_ref,
                       ocmp_ref, sel_ref, zl_ref, tok_ref, *, tq, n_cmp):
    i = pl.program_id(1)
    nch = zl_ref.shape[0]

    @pl.when(i == 0)
    def _():
        for b in range(CMP_STRIDE):
            zl_ref[:, b * LANE:(b + 1) * LANE] = cmp_ref[pl.ds(b, nch, stride=CMP_STRIDE), :]
        zl = zl_ref[...]
        tok = _compress((zl + pa_ref[...]).astype(BF16), (zl + pb_ref[...]).astype(BF16), w1a_ref, w1b_ref, w2_ref)
        tok_ref[...] = tok.astype(BF16)

    qs = jnp.concatenate([q_ref[:, hh * LANE:(hh + 1) * LANE] for hh in range(N_HEADS)], axis=0)
    qpos = i * tq + lax.broadcasted_iota(jnp.int32, (1, tq, 1), 1)
    o, imp = _cmp_attend(qs, tok_ref[...], qpos, n_cmp, 0)
    for hh in range(N_HEADS):
        ocmp_ref[:, hh * LANE:(hh + 1) * LANE] = o[hh * tq:(hh + 1) * tq]
    sel_ref[...] = _select_blocks(imp, mslc_ref, qpos[0] >> 6).astype(BF16)


def _slc_matrix(n_rows, n_cmp, n_cols, n_sel):
    a, r = SEL_BLOCK // CMP_STRIDE, CMP_BLOCK // CMP_STRIDE
    n = np.arange(n_rows)[:, None]
    j = np.arange(n_cols)[None, :]
    ok = (n >= a * j - (r - 1)) & (n <= a * j + a - 1) & (n < n_cmp) & (j < n_sel)
    return jnp.asarray(ok.astype(np.float32))


def _nsa_select_prompt(cmpf, qnsa, wl, *, n_seq, seq, tq=128):
    nch = seq // CMP_STRIDE
    n_cmp = (seq - CMP_BLOCK) // CMP_STRIDE + 1
    n_sel = max(-(-seq // SEL_BLOCK), SEL_TOPK)
    assert n_sel <= LANE and n_cmp + 1 == nch
    mslc = _slc_matrix(nch, n_cmp, LANE, n_sel)
    nq = seq // tq
    consts = (wl['pa'], wl['pb'], wl['w1a'], wl['w1b'], wl['w2'], mslc)
    full = lambda a: pl.BlockSpec(a.shape, lambda b, i: (0,) * a.ndim)
    q_map = lambda b, i: (b * nq + i, 0)
    return pl.pallas_call(
        functools.partial(_nsa_select_kernel, tq=tq, n_cmp=n_cmp),
        grid=(n_seq, nq),
        in_specs=[pl.BlockSpec((seq, LANE), lambda b, i: (b, 0)), pl.BlockSpec((tq, N_HEADS * LANE), q_map)]
        + [full(a) for a in consts],
        out_specs=[pl.BlockSpec((tq, N_HEADS * LANE), q_map), pl.BlockSpec((tq, LANE), q_map)],
        out_shape=[jax.ShapeDtypeStruct((n_seq * seq, N_HEADS * LANE), F32),
                   jax.ShapeDtypeStruct((n_seq * seq, LANE), BF16)],
        scratch_shapes=[pltpu.VMEM((nch, CMP_STRIDE * LANE), F32), pltpu.VMEM((nch, LANE), BF16)],
        compiler_params=_params(("parallel", "arbitrary")),
        name="nsa_select",
    )(cmpf, qnsa, *consts)


def _moba_select_kernel(rows_ref, q_ref, sel_ref, means_ref, *, tq, n_blk):
    i = pl.program_id(1)

    @pl.when(i == 0)
    def _():
        k = rows_ref[:, 0:LANE]
        means = jnp.sum(k.reshape(n_blk, MOBA_BLOCK, LANE), axis=1) * (1.0 / MOBA_BLOCK)
        means_ref[...] = jnp.zeros(means_ref.shape, BF16)
        means_ref[0:n_blk, :] = means.astype(BF16)

    qs = jnp.concatenate([q_ref[:, hh * LANE:(hh + 1) * LANE] for hh in range(N_HEADS)], axis=0)
    gsc = _dot_nt(qs, means_ref[...])
    rows = N_HEADS * tq
    cur = (i * tq + (lax.broadcasted_iota(jnp.int32, (rows, 1), 0) & (tq - 1))) >> 8
    jb = lax.broadcasted_iota(jnp.int32, (1, LANE), 1)
    sel = _topk_mask(jnp.where(jb < cur, gsc, NEG), MOBA_TOPK, jb.astype(F32))
    sel = jnp.where(jb < cur, sel, 0.0).astype(BF16)
    for hh in range(N_HEADS):
        sel_ref[:, hh * LANE:(hh + 1) * LANE] = sel[hh * tq:(hh + 1) * tq]


def _moba_select_prompt(mobaf, qmoba, *, n_seq, seq, tq=128):
    n_blk = seq // MOBA_BLOCK
    assert seq % MOBA_BLOCK == 0 and MOBA_TOPK <= n_blk <= 16
    nq = seq // tq
    q_map = lambda b, i: (b * nq + i, 0)
    return pl.pallas_call(
        functools.partial(_moba_select_kernel, tq=tq, n_blk=n_blk),
        grid=(n_seq, nq),
        in_specs=[pl.BlockSpec((seq, 2 * LANE), lambda b, i: (b, 0)), pl.BlockSpec((tq, N_HEADS * LANE), q_map)],
        out_specs=pl.BlockSpec((tq, N_HEADS * LANE), q_map),
        out_shape=jax.ShapeDtypeStruct((n_seq * seq, N_HEADS * LANE), BF16),
        scratch_shapes=[pltpu.VMEM((LANE, LANE), BF16)],
        compiler_params=_params(("parallel", "arbitrary")),
        name="moba_select",
    )(mobaf, qmoba)


def _softmax2(sp, sn, mn):
    sn = jnp.where(mn, sn, NEG)
    m = jnp.maximum(jnp.max(sp, axis=-1, keepdims=True), jnp.max(sn, axis=-1, keepdims=True))
    ep = jnp.exp(sp - m)
    en = jnp.where(mn, jnp.exp(sn - m), 0.0)
    den = jnp.sum(ep, axis=-1, keepdims=True) + jnp.sum(en, axis=-1, keepdims=True)
    return ep, en, den


def _new_mask(n_new_pad):
    qi = lax.broadcasted_iota(jnp.int32, (4 * N_HEADS, 1), 0) >> 3
    kj = lax.broadcasted_iota(jnp.int32, (1, n_new_pad), 1)
    return kj <= qi


def _mla_sample_kernel(pt_ref, qs_ref, new_ref, *rest, group, n_steps, chain):
    pages, (o_ref, m_ref, l_ref, acc_ref) = rest[:group], rest[group:]
    s = pl.program_id(1)

    @pl.when(s == 0)
    def _():
        m_ref[...] = jnp.full(m_ref.shape, MASK_FLOOR, F32)
        l_ref[...] = jnp.zeros(l_ref.shape, F32)
        acc_ref[...] = jnp.zeros(acc_ref.shape, F32)

    qs = qs_ref[...]
    zpad = jnp.zeros((MLA_WP - MLA_W, chain * PAGE), BF16)
    for c in range(group // chain):
        kt = jnp.concatenate([pages[c * chain + g][...].astype(BF16) for g in range(chain)], axis=1)
        sc = _dot(qs, jnp.concatenate([kt, zpad], axis=0)) * MLA_SCALE
        m_old = m_ref[c]
        m_new = jnp.maximum(m_old, jnp.max(sc, axis=-1, keepdims=True))
        p = jnp.exp(sc - m_new)
        alpha = jnp.exp(m_old - m_new)
        l_ref[c] = alpha * l_ref[c] + jnp.sum(p, axis=-1, keepdims=True)
        acc_ref[c] = alpha * acc_ref[c] + _dot_nt(p.astype(BF16), kt[0:LANE, :])
        m_ref[c] = m_new

    @pl.when(s == n_steps - 1)
    def _():
        new = new_ref[...]
        mn = _new_mask(new.shape[0])
        sn = jnp.where(mn, _dot_nt(qs, new) * MLA_SCALE, NEG)
        ms = m_ref[...]
        m_fin = jnp.maximum(jnp.max(ms, axis=0), jnp.max(sn, axis=-1, keepdims=True))
        en = jnp.where(mn, jnp.exp(sn - m_fin), 0.0)
        w = jnp.exp(ms - m_fin[None])
        den = jnp.sum(w * l_ref[...], axis=0) + jnp.sum(en, axis=-1, keepdims=True)
        o_ref[...] = (jnp.sum(w * acc_ref[...], axis=0) + _dot(en.astype(BF16), new[:, 0:LANE])) / den


def _page_specs(layer, n_pages, group, feat):
    def spec(g):
        return pl.BlockSpec((None, None, feat, PAGE),
                            lambda b, s, pt: (layer, pt[b * n_pages + s * group + g], 0, 0))
    return [spec(g) for g in range(group)]


def _mla_sample(pt, qs, new, cache_t, layer, *, group=64, chain=16):
    n_seq, n_pages = qs.shape[0], pt.shape[0] // qs.shape[0]
    group = min(group, n_pages)
    n_steps = n_pages // group
    nc = group // chain
    rows = 4 * N_HEADS
    per_seq = lambda a: pl.BlockSpec((None,) + a.shape[1:], lambda b, s, pt: (b, 0, 0))
    return pl.pallas_call(
        functools.partial(_mla_sample_kernel, group=group, n_steps=n_steps, chain=chain),
        grid_spec=pltpu.PrefetchScalarGridSpec(
            num_scalar_prefetch=1, grid=(n_seq, n_steps),
            in_specs=[per_seq(qs), per_seq(new)] + _page_specs(layer, n_pages, group, MLA_W),
            out_specs=pl.BlockSpec((None, rows, LANE), lambda b, s, pt: (b, 0, 0)),
            scratch_shapes=[pltpu.VMEM((nc, rows, 1), F32), pltpu.VMEM((nc, rows, 1), F32),
                            pltpu.VMEM((nc, rows, LANE), F32)]),
        out_shape=jax.ShapeDtypeStruct((n_seq, rows, LANE), F32),
        compiler_params=_params(("arbitrary", "arbitrary")),
        name="mla_sample",
    )(pt, qs, new, *([cache_t] * group))


def _moba_sample_kernel(pt_ref, qs_ref, new_ref, *rest, group, n_steps, n_blk):
    pages, (o_ref, gblk_ref, mblk_ref, lblk_ref, oblk_ref) = rest[:group], rest[group:]
    s = pl.program_id(1)
    lane = lax.broadcasted_iota(jnp.int32, (1, LANE), 1)
    ppb = MOBA_BLOCK // PAGE
    bps = group // ppb

    @pl.when(s == 0)
    def _():
        gblk_ref[...] = jnp.zeros(gblk_ref.shape, F32)
        mblk_ref[...] = jnp.full(mblk_ref.shape, NEG, F32)
        lblk_ref[...] = jnp.zeros(lblk_ref.shape, F32)

    qs = qs_ref[...]
    gacc, macc, lacc = gblk_ref[...], mblk_ref[...], lblk_ref[...]
    for j in range(bps):
        blk = s * bps + j
        xs = [pages[j * ppb + t][...] for t in range(ppb)]
        kt = jnp.concatenate([x[0:LANE, :].astype(BF16) for x in xs], axis=1)
        vt = jnp.concatenate([x[LANE:2 * LANE, :].astype(BF16) for x in xs], axis=1)
        sc = _dot(qs, kt) * ATTN_SCALE
        mb = jnp.max(sc, axis=-1, keepdims=True)
        p = jnp.exp(sc - mb)
        here = lane == blk
        gacc = jnp.where(here, jnp.sum(sc, axis=-1, keepdims=True) * (1.0 / (MOBA_BLOCK * ATTN_SCALE)), gacc)
        macc = jnp.where(here, mb, macc)
        lacc = jnp.where(here, jnp.sum(p, axis=-1, keepdims=True), lacc)
        oblk_ref[blk] = _dot_nt(p.astype(BF16), vt)
    gblk_ref[...], mblk_ref[...], lblk_ref[...] = gacc, macc, lacc

    @pl.when(s == n_steps - 1)
    def _():
        new = new_ref[...]
        ok = lane < n_blk
        sel = _topk_mask(jnp.where(ok, gblk_ref[...], NEG), MOBA_TOPK, lane.astype(F32))
        pick = ok & (sel > 0.5)
        mn = _new_mask(new.shape[0])
        sn = jnp.where(mn, _dot_nt(qs, new[:, 0:LANE]) * ATTN_SCALE, NEG)
        mblk = mblk_ref[...]
        m_fin = jnp.maximum(jnp.max(jnp.where(pick, mblk, NEG), axis=-1, keepdims=True),
                            jnp.max(sn, axis=-1, keepdims=True))
        w = jnp.where(pick, jnp.exp(mblk - m_fin), 0.0)
        en = jnp.where(mn, jnp.exp(sn - m_fin), 0.0)
        den = jnp.sum(w * lblk_ref[...], axis=-1, keepdims=True) + jnp.sum(en, axis=-1, keepdims=True)
        o = _dot(en.astype(BF16), new[:, LANE:2 * LANE])
        for b in range(n_blk):
            o = o + w[:, b:b + 1] * oblk_ref[b]
        o_ref[...] = o / den


def _moba_sample(pt, qs, new, cache_t, layer, *, group=32):
    n_seq, n_pages = qs.shape[0], pt.shape[0] // qs.shape[0]
    n_blk = n_pages * PAGE // MOBA_BLOCK
    assert n_blk <= LANE and n_pages % 2 == 0
    group = min(group, n_pages)
    n_steps = n_pages // group
    rows = 4 * N_HEADS
    per_seq = lambda a: pl.BlockSpec((None,) + a.shape[1:], lambda b, s, pt: (b, 0, 0))
    return pl.pallas_call(
        functools.partial(_moba_sample_kernel, group=group, n_steps=n_steps, n_blk=n_blk),
        grid_spec=pltpu.PrefetchScalarGridSpec(
            num_scalar_prefetch=1, grid=(n_seq, n_steps),
            in_specs=[per_seq(qs), per_seq(new)] + _page_specs(layer, n_pages, group, 2 * LANE),
            out_specs=pl.BlockSpec((None, rows, LANE), lambda b, s, pt: (b, 0, 0)),
            scratch_shapes=[pltpu.VMEM((rows, LANE), F32), pltpu.VMEM((rows, LANE), F32),
                            pltpu.VMEM((rows, LANE), F32), pltpu.VMEM((n_blk, rows, LANE), F32)]),
        out_shape=jax.ShapeDtypeStruct((n_seq, rows, LANE), F32),
        compiler_params=_params(("arbitrary", "arbitrary")),
        name="moba_sample",
    )(pt, qs, new, *([cache_t] * group))


def _nsa_sample_kernel(pt_ref, qs_ref, newsel_ref, newwin_ref, win_ref, gate_ref, pat_ref, pbt_ref, perm_ref,
                       w1a_ref, w1b_ref, w2_ref, mslc_ref, shift_ref, expand_ref, *rest,
                       group, n_steps, n_cmp, past_len):
    cmp_pages, sel_pages = rest[:group], rest[group:2 * group]
    o_ref, zla_ref, zlb_ref, ocmp_ref, selm_ref, m_ref, l_ref, acc_ref = rest[2 * group:]
    s = pl.program_id(1)
    qs = qs_ref[...]
    nrow = qs.shape[0]

    @pl.when(s < n_steps)
    def _():
        perm = perm_ref[...]
        for g in range(0, group, 2):
            x = jnp.concatenate([cmp_pages[g][...], cmp_pages[g + 1][...]], axis=1)
            ra = _dot_nt(perm, (x + pat_ref[...]).astype(BF16)).astype(BF16)
            rb = _dot_nt(perm, (x + pbt_ref[...]).astype(BF16)).astype(BF16)
            r0 = pl.multiple_of((s * group + g) * 8, 16)
            for bb in range(CMP_STRIDE):
                zla_ref[pl.ds(r0, 16), bb * LANE:(bb + 1) * LANE] = ra[bb * 16:(bb + 1) * 16, :]
                zlb_ref[pl.ds(r0, 16), bb * LANE:(bb + 1) * LANE] = rb[bb * 16:(bb + 1) * 16, :]

    @pl.when(s == n_steps - 1)
    def _():
        qidx = lax.broadcasted_iota(jnp.int32, (nrow // N_HEADS, 1, 1), 0)
        tok = _compress(zla_ref[...], zlb_ref[...], w1a_ref, w1b_ref, w2_ref).astype(BF16)
        o_cmp, imp = _cmp_attend(qs, tok, past_len + qidx, n_cmp, 1)
        imp = jnp.broadcast_to(imp[:, None, :], (nrow // N_HEADS, N_HEADS, imp.shape[1])).reshape(nrow, imp.shape[1])
        cur = (past_len + (lax.broadcasted_iota(jnp.int32, (nrow, 1), 0) >> 3)) >> 6
        ocmp_ref[...] = o_cmp
        selm_ref[...] = _select_blocks(imp, mslc_ref, cur).astype(BF16)
        m_ref[...] = jnp.full(m_ref.shape, MASK_FLOOR, F32)
        l_ref[...] = jnp.zeros(l_ref.shape, F32)
        acc_ref[...] = jnp.zeros(acc_ref.shape, F32)

    @pl.when(s >= n_steps)
    def _():
        st = jnp.concatenate([sel_pages[g][...].astype(BF16) for g in range(group)], axis=1)
        picked = _dot(_dot(selm_ref[...], shift_ref[s - n_steps]).astype(BF16), expand_ref[...])
        sc = _dot(qs, st) * ATTN_SCALE + (picked - 1.0) * (-NEG)
        m_old = m_ref[...]
        m_new = jnp.maximum(m_old, jnp.max(sc, axis=-1, keepdims=True))
        p = jnp.exp(sc - m_new)
        alpha = jnp.exp(m_old - m_new)
        l_ref[...] = alpha * l_ref[...] + jnp.sum(p, axis=-1, keepdims=True)
        acc_ref[...] = alpha * acc_ref[...] + _dot_nt(p.astype(BF16), st)
        m_ref[...] = m_new

    @pl.when(s == 2 * n_steps - 1)
    def _():
        mn = _new_mask(newsel_ref.shape[0])
        new = newsel_ref[...]
        sn = jnp.where(mn, _dot_nt(qs, new) * ATTN_SCALE, NEG)
        m_fin = jnp.maximum(m_ref[...], jnp.max(sn, axis=-1, keepdims=True))
        en = jnp.where(mn, jnp.exp(sn - m_fin), 0.0)
        w = jnp.exp(m_ref[...] - m_fin)
        den = w * l_ref[...] + jnp.sum(en, axis=-1, keepdims=True)
        o_sel = (w * acc_ref[...] + _dot(en.astype(BF16), new)) / den
        o_cmp = ocmp_ref[...]
        wt = win_ref[...].astype(BF16)
        neww = newwin_ref[...]
        sw = _dot(qs, wt) * ATTN_SCALE
        vis = lax.broadcasted_iota(jnp.int32, (1, wt.shape[1]), 1) >= (lax.broadcasted_iota(jnp.int32, (nrow, 1), 0) >> 3)
        sw = jnp.where(vis, sw, NEG)
        ew, enw, denw = _softmax2(sw, _dot_nt(qs, neww) * ATTN_SCALE, mn)
        ew = jnp.where(vis, ew, 0.0)
        o_win = (_dot_nt(ew.astype(BF16), wt) + _dot(enw.astype(BF16), neww)) / denw
        gt = gate_ref[...]
        o_ref[...] = gt[:, 0:1] * o_cmp + gt[:, 1:2] * o_sel + gt[:, 2:3] * o_win


def _nsa_sample(pt, qs, newsel, newwin, win_t, gates, cmp_t, sel_t, wl, layer, *, group=16):
    n_seq, n_pages = qs.shape[0], pt.shape[0] // qs.shape[0]
    past_len = n_pages * PAGE
    group = min(group, n_pages)
    n_steps = n_pages // group
    nch = past_len // CMP_STRIDE
    length = past_len + 4
    n_cmp = (length - CMP_BLOCK) // CMP_STRIDE + 1
    n_sel = max(-(-length // SEL_BLOCK), SEL_TOPK)
    assert n_cmp + 1 == nch
    nsp = -(-n_sel // LANE) * LANE
    mslc = _slc_matrix(nch, n_cmp, nsp, n_sel)
    chunk = group * PAGE
    n_chunks = n_steps
    bpc = chunk // SEL_BLOCK
    shift = np.stack([(np.arange(nsp)[:, None] == c * bpc + np.arange(LANE)[None, :]) & (np.arange(LANE)[None, :] < bpc)
                      for c in range(n_chunks)]).astype(np.float32)
    expand = (np.arange(LANE)[:, None] == (np.arange(chunk)[None, :] // SEL_BLOCK)).astype(np.float32)
    r = np.arange(2 * PAGE)
    perm = (r[None, :] == (CMP_STRIDE * (r[:, None] % 16) + r[:, None] // 16)).astype(np.float32)
    shift, expand, perm = jnp.asarray(shift, BF16), jnp.asarray(expand, BF16), jnp.asarray(perm, BF16)
    consts = (wl['pat'], wl['pbt'], perm, wl['w1a'], wl['w1b'], wl['w2'], mslc, shift, expand)
    per_seq = lambda a: pl.BlockSpec((None,) + a.shape[1:], lambda b, s, pt: (b, 0, 0))
    full = lambda a: pl.BlockSpec(a.shape, lambda b, s, pt: (0,) * a.ndim)
    win_spec = pl.BlockSpec((None, None, LANE, win_t.shape[3]), lambda b, s, pt: (layer, b, 0, 0))
    rows = 4 * N_HEADS

    def page_specs(phase):
        def spec(g):
            def index(b, s, pt):
                step = jnp.minimum(s, n_steps - 1) if phase == 0 else jnp.maximum(s - n_steps, 0)
                return (layer, pt[b * n_pages + step * group + g], 0, 0)
            return pl.BlockSpec((None, None, LANE, PAGE), index)
        return [spec(g) for g in range(group)]

    return pl.pallas_call(
        functools.partial(_nsa_sample_kernel, group=group, n_steps=n_steps, n_cmp=n_cmp, past_len=past_len),
        grid_spec=pltpu.PrefetchScalarGridSpec(
            num_scalar_prefetch=1, grid=(n_seq, 2 * n_steps),
            in_specs=[per_seq(qs), per_seq(newsel), per_seq(newwin), win_spec, per_seq(gates)]
            + [full(a) for a in consts] + page_specs(0) + page_specs(1),
            out_specs=pl.BlockSpec((None, rows, LANE), lambda b, s, pt: (b, 0, 0)),
            scratch_shapes=[pltpu.VMEM((nch, CMP_STRIDE * LANE), BF16), pltpu.VMEM((nch, CMP_STRIDE * LANE), BF16),
                            pltpu.VMEM((rows, LANE), F32), pltpu.VMEM((rows, nsp), BF16),
                            pltpu.VMEM((rows, 1), F32), pltpu.VMEM((rows, 1), F32), pltpu.VMEM((rows, LANE), F32)]),
        out_shape=jax.ShapeDtypeStruct((n_seq, rows, LANE), F32),
        compiler_params=_params(("arbitrary", "arbitrary")),
        name="nsa_sample",
    )(pt, qs, newsel, newwin, win_t, gates, *consts, *([cmp_t] * group), *([sel_t] * group))


def _merge_kernel(x_ref, g_ref, olat_ref, ocmp_ref, osel_ref, owin_ref, gate_ref, omoba_ref,
                  wgm_ref, wuv_ref, wb0_ref, wb1_ref, wb2_ref, wout_ref, o_ref):
    x = x_ref[...]
    d = x.shape[1]
    h = _rms(x, g_ref[...]).astype(BF16)
    gm = jax.nn.sigmoid(_dot(h, wgm_ref[...]))
    up0 = _dot(_dot(olat_ref[...].astype(BF16), wuv_ref[...]).astype(BF16), wb0_ref[...])
    gt = gate_ref[...]
    parts = []
    for hh in range(N_HEADS):
        blk = slice(hh * LANE, (hh + 1) * LANE)
        parts.append(gt[:, hh:hh + 1] * ocmp_ref[:, blk] + gt[:, N_HEADS + hh:N_HEADS + hh + 1] * osel_ref[:, blk]
                     + gt[:, 2 * N_HEADS + hh:2 * N_HEADS + hh + 1] * owin_ref[:, blk])
    up1 = _dot(jnp.concatenate(parts, axis=1).astype(BF16), wb1_ref[...])
    up2 = _dot(omoba_ref[...].astype(BF16), wb2_ref[...])
    mix = gm[:, 0:d] * up0 + gm[:, d:2 * d] * up1 + gm[:, 2 * d:3 * d] * up2
    o_ref[...] = x + _dot(mix.astype(BF16), wout_ref[...])


def _merge(x, g, olat, ocmp, osel, owin, gates, omoba, wl, tm):
    t, d = x.shape
    row = lambda a: pl.BlockSpec((tm, a.shape[1]), lambda i: (i, 0))
    full = lambda a: pl.BlockSpec(a.shape, lambda i: (0,) * a.ndim)
    acts = (olat, ocmp, osel, owin, gates, omoba)
    consts = (wl['wgm'], wl['wuv'], wl['wb0'], wl['wb1'], wl['wb2'], wl['wout'])
    return pl.pallas_call(
        _merge_kernel,
        grid=(t // tm,),
        in_specs=[row(x), full(g)] + [row(a) for a in acts] + [full(a) for a in consts],
        out_specs=row(x),
        out_shape=jax.ShapeDtypeStruct((t, d), F32),
        compiler_params=_params(("parallel",)),
        name="merge",
    )(x, g, *acts, *consts)


def _ffn_kernel(x_ref, g_ref, gf_ref, wg_ref, wu_ref, wd_ref, o_ref, h_ref, acc_ref, *, nf, final):
    f = pl.program_id(1)

    @pl.when(f == 0)
    def _():
        h_ref[...] = _rms(x_ref[...], g_ref[...]).astype(BF16)
        acc_ref[...] = jnp.zeros(acc_ref.shape, F32)

    h = h_ref[...]
    act = jax.nn.silu(_dot(h, wg_ref[...])) * _dot(h, wu_ref[...])
    acc_ref[...] += _dot(act.astype(BF16), wd_ref[...])

    @pl.when(f == nf - 1)
    def _():
        y = x_ref[...] + acc_ref[...]
        o_ref[...] = _rms(y, gf_ref[...]) if final else y


def _ffn(x, g, gfinal, wgu, wd, *, final, tm=512, tf=512):
    t, d = x.shape
    tm = tm if t % tm == 0 else 256
    dff = wd.shape[0]
    nf = dff // tf
    return pl.pallas_call(
        functools.partial(_ffn_kernel, nf=nf, final=final),
        grid=(t // tm, nf),
        in_specs=[pl.BlockSpec((tm, d), lambda i, f: (i, 0)), pl.BlockSpec((1, d), lambda i, f: (0, 0)),
                  pl.BlockSpec((1, d), lambda i, f: (0, 0)),
                  pl.BlockSpec((d, tf), lambda i, f: (0, f)), pl.BlockSpec((d, tf), lambda i, f: (0, nf + f)),
                  pl.BlockSpec((tf, d), lambda i, f: (f, 0))],
        out_specs=pl.BlockSpec((tm, d), lambda i, f: (i, 0)),
        out_shape=jax.ShapeDtypeStruct((t, d), F32),
        scratch_shapes=[pltpu.VMEM((tm, d), BF16), pltpu.VMEM((tm, d), F32)],
        compiler_params=_params(("parallel", "arbitrary")),
        name="ffn",
    )(x, g, gfinal, wgu, wgu, wd)


def _moe_kernel(x_ref, g_ref, gf_ref, wr_ref, wg_ref, wu_ref, wd_ref, o_ref, h_ref, gate_ref, acc_ref,
                *, ne, nf, final):
    e, f = pl.program_id(1), pl.program_id(2)
    lane = lax.broadcasted_iota(jnp.int32, (1, LANE), 1)

    @pl.when((e == 0) & (f == 0))
    def _():
        h = _rms(x_ref[...], g_ref[...]).astype(BF16)
        h_ref[...] = h
        acc_ref[...] = jnp.zeros(acc_ref.shape, F32)
        lanef = lane.astype(F32)
        logits = jnp.where(lane < ne, _dot(h, wr_ref[...]), NEG)
        v1 = jnp.max(logits, axis=-1, keepdims=True)
        i1 = jnp.min(jnp.where(logits == v1, lanef, 1e9), axis=-1, keepdims=True)
        rest = jnp.where(lanef == i1, NEG, logits)
        v2 = jnp.max(rest, axis=-1, keepdims=True)
        i2 = jnp.min(jnp.where(rest == v2, lanef, 1e9), axis=-1, keepdims=True)
        e2 = jnp.exp(v2 - v1)
        gate_ref[...] = jnp.where(lanef == i1, 1.0 / (1.0 + e2), 0.0) + jnp.where(lanef == i2, e2 / (1.0 + e2), 0.0)

    h = h_ref[...]
    act = jax.nn.silu(_dot(h, wg_ref[...])) * _dot(h, wu_ref[...])
    ge = jnp.sum(jnp.where(lane == e, gate_ref[...], 0.0), axis=-1, keepdims=True)
    acc_ref[...] += ge * _dot(act.astype(BF16), wd_ref[...])

    @pl.when((e == ne - 1) & (f == nf - 1))
    def _():
        y = x_ref[...] + acc_ref[...]
        o_ref[...] = _rms(y, gf_ref[...]) if final else y


def _moe(x, g, gfinal, router, wgu, wd, *, final, tm=512, tf=512):
    t, d = x.shape
    tm = tm if t % tm == 0 else 256
    ne, dff = wd.shape[0], wd.shape[1]
    nf = dff // tf
    return pl.pallas_call(
        functools.partial(_moe_kernel, ne=ne, nf=nf, final=final),
        grid=(t // tm, ne, nf),
        in_specs=[pl.BlockSpec((tm, d), lambda i, e, f: (i, 0)), pl.BlockSpec((1, d), lambda i, e, f: (0, 0)),
                  pl.BlockSpec((1, d), lambda i, e, f: (0, 0)), pl.BlockSpec((d, LANE), lambda i, e, f: (0, 0)),
                  pl.BlockSpec((None, d, tf), lambda i, e, f: (e, 0, f)),
                  pl.BlockSpec((None, d, tf), lambda i, e, f: (e, 0, nf + f)),
                  pl.BlockSpec((None, tf, d), lambda i, e, f: (e, f, 0))],
        out_specs=pl.BlockSpec((tm, d), lambda i, e, f: (i, 0)),
        out_shape=jax.ShapeDtypeStruct((t, d), F32),
        scratch_shapes=[pltpu.VMEM((tm, d), BF16), pltpu.VMEM((tm, LANE), F32), pltpu.VMEM((tm, d), F32)],
        compiler_params=_params(("parallel", "arbitrary", "arbitrary")),
        name="moe",
    )(x, g, gfinal, router, wgu, wgu, wd)


def _rope_tables(pos):
    posf = pos.astype(F32)[:, None]

    def cs(rot, theta):
        inv = 1.0 / (theta ** (jnp.arange(0, rot, 2, dtype=F32) / rot))
        ang = posf * inv[None, :]
        return jnp.cos(ang), jnp.sin(ang)

    t = pos.shape[0]
    cp, sp = cs(2 * ROT_HALF, ROPE_THETA)
    one, zero = jnp.ones((t, HEAD_DIM - 2 * ROT_HALF), F32), jnp.zeros((t, HEAD_DIM - 2 * ROT_HALF), F32)
    z8 = jnp.zeros((t, ROT_HALF), F32)
    cd = jnp.tile(jnp.concatenate([cp, cp, one], 1), (1, 2))
    s1d = jnp.tile(jnp.concatenate([-sp, z8, zero], 1), (1, 2))
    s2d = jnp.tile(jnp.concatenate([z8, sp, zero], 1), (1, 2))
    cmm, smm = cs(MLA_ROPE, MLA_THETA)
    hm = MLA_ROPE // 2
    zm, onem = jnp.zeros((t, hm), F32), jnp.ones((t, LANE - MLA_ROPE), F32)
    zrest = jnp.zeros((t, LANE - MLA_ROPE), F32)
    cm = jnp.concatenate([cmm, cmm, onem], 1)
    s1m = jnp.concatenate([-smm, zm, zrest], 1)
    s2m = jnp.concatenate([zm, smm, zrest], 1)
    return cd, s1d, s2d, cm, s1m, s2m


def _pad_heads(w, offsets):
    out = jnp.zeros((w.shape[0], N_HEADS, LANE), w.dtype)
    for hh in range(N_HEADS):
        out = out.at[:, hh, offsets[hh]:offsets[hh] + HEAD_DIM].set(w[:, hh])
    return out.reshape(w.shape[0], N_HEADS * LANE)


def _layer_weights(l, w_in, mla_q_norm, mla_kv_norm, mla_w_uq, mla_w_uk, mla_w_uv, nsa_cmp_pos, nsa_cmp_w1,
                   nsa_cmp_w2, w_branch, w_out):
    d = w_in.shape[1]
    w = w_in[l]
    sizes = (MLA_Q_LORA, MLA_KV_LORA, MLA_ROPE, N_HEADS * HEAD_DIM, 2 * HEAD_DIM, 2 * HEAD_DIM, 2 * HEAD_DIM,
             3 * N_HEADS, N_HEADS * HEAD_DIM, 2 * MOBA_KVH * HEAD_DIM, 3 * d)
    pts = [int(v) for v in np.cumsum(sizes)[:-1]]
    cq, ckv, kr, qb, kvc, kvs, kvw, gb, qc, kvm, gm = jnp.split(w, pts, axis=1)
    padl = lambda a: jnp.pad(a, ((0, 0), (0, LANE - a.shape[1])))
    moba_off = [HEAD_DIM * (hh // (N_HEADS // MOBA_KVH)) for hh in range(N_HEADS)]
    wp = jnp.concatenate([cq, ckv, padl(kr), _pad_heads(qb.reshape(d, N_HEADS, HEAD_DIM), [0] * N_HEADS), kvc, kvs,
                          kvw, padl(gb), _pad_heads(qc.reshape(d, N_HEADS, HEAD_DIM), moba_off), kvm], axis=1)
    assert wp.shape[1] == C_END
    eye_h, eye_2 = jnp.eye(N_HEADS, dtype=F32), jnp.eye(2, dtype=F32)
    uq = mla_w_uq[l]
    wuqr = jnp.pad(uq[:, :, MLA_NOPE:], ((0, 0), (0, 0), (0, LANE - MLA_ROPE))).reshape(MLA_Q_LORA, N_HEADS * LANE)
    w1 = nsa_cmp_w1[l]
    half = lambda a: jnp.einsum('cbde,cg->bcdge', a, eye_2).reshape(CMP_STRIDE * LANE, 2 * CMP_HIDDEN)
    pos = nsa_cmp_pos[l].reshape(CMP_BLOCK, LANE)
    pa, pb = pos[:CMP_STRIDE], pos[CMP_STRIDE:]
    wb = w_branch[l]
    wb1 = jnp.pad(wb[1].reshape(N_HEADS, HEAD_DIM, d), ((0, 0), (HEAD_DIM, 0), (0, 0))).reshape(N_HEADS * LANE, d)
    wb2 = _pad_heads(wb[2].reshape(N_HEADS, HEAD_DIM, d).transpose(2, 0, 1), moba_off).T
    bf = lambda a: a.astype(BF16)
    return dict(
        wp=bf(wp), wgm=bf(gm), qn=mla_q_norm[l][None], kvn=mla_kv_norm[l][None],
        wuqn=bf(uq[:, :, :MLA_NOPE].reshape(MLA_Q_LORA, N_HEADS * MLA_NOPE)), wuqr=bf(wuqr),
        wuk=bf(jnp.einsum('chd,hg->hdgc', mla_w_uk[l], eye_h).reshape(N_HEADS * MLA_NOPE, N_HEADS * MLA_KV_LORA)),
        wuv=bf(jnp.einsum('chv,hg->hcgv', mla_w_uv[l], eye_h).reshape(N_HEADS * MLA_KV_LORA, N_HEADS * HEAD_DIM)),
        w1a=bf(half(w1[:, :CMP_STRIDE])), w1b=bf(half(w1[:, CMP_STRIDE:])),
        w2=bf(jnp.einsum('ced,cg->cegd', nsa_cmp_w2[l], eye_2).reshape(2 * CMP_HIDDEN, LANE)),
        pa=pa.reshape(1, CMP_STRIDE * LANE), pb=pb.reshape(1, CMP_STRIDE * LANE),
        pat=jnp.tile(pa.T, (1, 2 * PAGE // CMP_STRIDE)), pbt=jnp.tile(pb.T, (1, 2 * PAGE // CMP_STRIDE)),
        wb0=bf(wb[0]), wb1=bf(wb1), wb2=bf(wb2), wout=bf(w_out[l]))


def kernel(x_prompt, x_sample, cache_mla, cache_nsa_cmp, cache_nsa_sel, state_nsa_win, cache_moba, page_table, norm_mix, w_in, mla_q_norm, mla_kv_norm, mla_w_uq, mla_w_uk, mla_w_uv, nsa_cmp_pos, nsa_cmp_w1, nsa_cmp_w2, w_branch, w_out, norm_ffn, ffn_w_gate_up, ffn_w_down, moe_router, moe_w_gate_up, moe_w_down, norm_final):
    nb, seq, d = x_prompt.shape
    ns, nn, _ = x_sample.shape
    depth = w_in.shape[0]
    n_pages = page_table.shape[1]
    past_len = n_pages * PAGE
    tp, ts = nb * seq, ns * nn
    assert nn == 4 and ts % 256 == 0 and seq % 512 == 0

    x = jnp.concatenate([x_prompt.reshape(tp, d), x_sample.reshape(ts, d)], axis=0)
    pos = jnp.concatenate([jnp.tile(jnp.arange(seq), nb), jnp.tile(past_len + jnp.arange(nn), ns)])
    tabs = _rope_tables(pos)
    pt = page_table.reshape(-1)
    npool = cache_mla.shape[1]
    mla_t = jnp.transpose(cache_mla, (0, 1, 3, 2))
    cmp_t = jnp.transpose(cache_nsa_cmp, (0, 1, 3, 4, 2)).reshape(depth, npool, LANE, PAGE)
    sel_t = jnp.transpose(cache_nsa_sel, (0, 1, 3, 4, 2)).reshape(depth, npool, LANE, PAGE)
    moba_t = jnp.transpose(cache_moba, (0, 1, 3, 4, 5, 2)).reshape(depth, npool, 2 * LANE, PAGE)
    win_t = jnp.transpose(state_nsa_win, (0, 1, 3, 4, 2)).reshape(depth, ns, LANE, state_nsa_win.shape[2])

    def stack_q(a):
        return a[tp:].reshape(ns, nn * N_HEADS, a.shape[1] // N_HEADS)

    def new_rows(a):
        return jnp.pad(a[tp:].reshape(ns, nn, a.shape[1]), ((0, 0), (0, 16 - nn), (0, 0)))

    outs = [[] for _ in range(10)]
    for l in range(depth):
        wl = _layer_weights(l, w_in, mla_q_norm, mla_kv_norm, mla_w_uq, mla_w_uk, mla_w_uv, nsa_cmp_pos,
                            nsa_cmp_w1, nsa_cmp_w2, w_branch, w_out)
        g_mix = norm_mix[l][None]
        (qmla, mlaf, mlab, qnsa, cmpf, self_, selb, winf, winb, gates, qmoba, mobaf, mobab) = _project(
            x, g_mix, wl, tabs, 256)

        olat_p = _prompt_attention(qmla, mlab, None, mode='mla', n_seq=nb, seq=seq)
        ocmp_p, selmask = _nsa_select_prompt(cmpf, qnsa, wl, n_seq=nb, seq=seq)
        osel_p = _prompt_attention(qnsa, selb, selmask, mode='sel', n_seq=nb, seq=seq)
        owin_p = _prompt_attention(qnsa, winb, None, mode='win', n_seq=nb, seq=seq)
        mobasel = _moba_select_prompt(mobaf, qmoba, n_seq=nb, seq=seq)
        omoba_p = _prompt_attention(qmoba, mobab, mobasel, mode='moba', n_seq=nb, seq=seq)

        olat_s = _mla_sample(pt, stack_q(qmla), new_rows(mlab), mla_t, l)
        gs = gates[tp:, 0:3 * N_HEADS].reshape(ns, nn, 3, N_HEADS).transpose(0, 1, 3, 2).reshape(ns, nn * N_HEADS, 3)
        gs = jnp.pad(gs, ((0, 0), (0, 0), (0, LANE - 3)))
        onsa_s = _nsa_sample(pt, stack_q(qnsa), new_rows(selb), new_rows(winb), win_t, gs, cmp_t, sel_t, wl, l)
        omoba_s = _moba_sample(pt, stack_q(qmoba), new_rows(mobab), moba_t, l)

        unstack = lambda o: o.reshape(ts, N_HEADS * LANE)
        olat = jnp.concatenate([olat_p, unstack(olat_s)], axis=0)
        ocmp = jnp.concatenate([ocmp_p, unstack(onsa_s)], axis=0)
        zeros_s = jnp.zeros((ts, N_HEADS * LANE), F32)
        osel = jnp.concatenate([osel_p, zeros_s], axis=0)
        owin = jnp.concatenate([owin_p, zeros_s], axis=0)
        unit = jnp.concatenate([jnp.ones((ts, N_HEADS), F32), jnp.zeros((ts, LANE - N_HEADS), F32)], axis=1)
        gates_m = jnp.concatenate([gates[:tp], unit], axis=0)
        omoba = jnp.concatenate([omoba_p, unstack(omoba_s)], axis=0)

        x = _merge(x, g_mix, olat, ocmp, osel, owin, gates_m, omoba, wl, 256)
        last = l == depth - 1
        g_ffn, g_fin = norm_ffn[l][None], norm_final[None]
        if l % 2 == 0:
            x = _ffn(x, g_ffn, g_fin, ffn_w_gate_up[l // 2].astype(BF16), ffn_w_down[l // 2].astype(BF16), final=last)
        else:
            router = jnp.pad(moe_router[l // 2], ((0, 0), (0, LANE - N_EXPERTS))).astype(BF16)
            x = _moe(x, g_ffn, g_fin, router, moe_w_gate_up[l // 2].astype(BF16), moe_w_down[l // 2].astype(BF16),
                     final=last)

        win_prompt = winf[:tp].reshape(nb, seq, 2, HEAD_DIM)[:, seq - min(NSA_WINDOW, seq):]
        win_new = winf[tp:].reshape(ns, nn, 2, HEAD_DIM)
        rows = (mlaf[:tp].reshape(nb, seq, MLA_W), mlaf[tp:].reshape(ns, nn, MLA_W),
                cmpf[:tp].reshape(nb, seq, 2, HEAD_DIM), cmpf[tp:].reshape(ns, nn, 2, HEAD_DIM),
                self_[:tp].reshape(nb, seq, 2, HEAD_DIM), self_[tp:].reshape(ns, nn, 2, HEAD_DIM),
                win_prompt, jnp.concatenate([state_nsa_win[l], win_new], axis=1)[:, nn:],
                mobaf[:tp].reshape(nb, seq, 2, MOBA_KVH, HEAD_DIM), mobaf[tp:].reshape(ns, nn, 2, MOBA_KVH, HEAD_DIM))
        for acc, r in zip(outs, rows):
            acc.append(r)

    y_prompt = x[:tp].reshape(nb, seq, d)
    y_sample = x[tp:].reshape(ns, nn, d)
    return (y_prompt, y_sample) + tuple(jnp.stack(o) for o in outs)
```

```python
import functools

import numpy as np
import jax
import jax.numpy as jnp
from jax import lax
from jax.experimental import pallas as pl
from jax.experimental.pallas import tpu as pltpu

F32 = jnp.float32
BF16 = jnp.bfloat16
NEG = -1e30
MASK_FLOOR = -1e29
RMS_EPS = 1e-6

N_HEADS = 8
HEAD_DIM = 64
LANE = 128
ROT_HALF = 8
ROPE_THETA = 500000.0
ATTN_SCALE = HEAD_DIM ** -0.5
MLA_Q_LORA = 256
MLA_KV_LORA = 128
MLA_NOPE = 64
MLA_ROPE = 32
MLA_THETA = 10000.0
MLA_SCALE = (MLA_NOPE + MLA_ROPE) ** -0.5
MLA_W = MLA_KV_LORA + MLA_ROPE
MLA_WP = 256
CMP_BLOCK = 32
CMP_STRIDE = 16
CMP_HIDDEN = 256
SEL_BLOCK = 64
SEL_TOPK = 16
NSA_WINDOW = 512
NSA_FORCE = 1000.0
MOBA_BLOCK = 256
MOBA_TOPK = 3
MOBA_KVH = 2
N_EXPERTS = 8
PAGE = 128

VMEM_LIMIT = 56 * 1024 * 1024

C_CQ, C_CKV, C_KR, C_QB, C_CMP, C_SEL, C_WIN, C_GB, C_QC, C_KVM, C_END = (
    0, 256, 384, 512, 1536, 1664, 1792, 1920, 2048, 3072, 3328)


def _dot(a, b):
    return jnp.dot(a, b, preferred_element_type=F32)


def _dot_nt(a, b):
    return lax.dot_general(a, b, (((1,), (1,)), ((), ())), preferred_element_type=F32)


def _rms(x, g):
    return x * lax.rsqrt(jnp.mean(x * x, axis=-1, keepdims=True) + RMS_EPS) * g


def _rope(v, c, s1, s2, half):
    return v * c + pltpu.roll(v, LANE - half, 1) * s1 + pltpu.roll(v, half, 1) * s2


def _topk_mask(score, k, jbf):
    sel = jnp.zeros(score.shape, F32)
    sc = score
    for _ in range(k):
        mx = jnp.max(sc, axis=-1, keepdims=True)
        idx = jnp.min(jnp.where(sc == mx, jbf, 1e9), axis=-1, keepdims=True)
        pick = jbf == idx
        sel = jnp.where(pick, 1.0, sel)
        sc = jnp.where(pick, -jnp.inf, sc)
    return sel


def _params(sem):
    return pltpu.CompilerParams(dimension_semantics=sem, vmem_limit_bytes=VMEM_LIMIT)


def _proj_kernel(x_ref, g_ref, wp_ref, qn_ref, kvn_ref, wuqn_ref, wuqr_ref, wuk_ref,
                 cd_ref, s1d_ref, s2d_ref, cm_ref, s1m_ref, s2m_ref,
                 qmla_ref, mlaf_ref, mlab_ref, qnsa_ref, cmpf_ref, self_ref, selb_ref, winf_ref, winb_ref,
                 gate_ref, qmoba_ref, mobaf_ref, mobab_ref):
    h = _rms(x_ref[...], g_ref[...]).astype(BF16)

    def seg(a, b):
        return _dot(h, wp_ref[:, a:b])

    lo = lax.broadcasted_iota(jnp.int32, (1, LANE), 1) < HEAD_DIM
    cd, s1d, s2d = cd_ref[...], s1d_ref[...], s2d_ref[...]
    cs, s1s, s2s = jnp.where(lo, cd, 1.0), jnp.where(lo, s1d, 0.0), jnp.where(lo, s2d, 0.0)
    cm, s1m, s2m = cm_ref[...], s1m_ref[...], s2m_ref[...]

    cqn = _rms(seg(C_CQ, C_CKV), qn_ref[...]).astype(BF16)
    qlat = _dot(_dot(cqn, wuqn_ref[...]).astype(BF16), wuk_ref[...])
    qr = _dot(cqn, wuqr_ref[...])
    for hh in range(N_HEADS):
        blk = slice(hh * LANE, (hh + 1) * LANE)
        qmla_ref[:, hh * MLA_WP:hh * MLA_WP + LANE] = qlat[:, blk].astype(BF16)
        qmla_ref[:, hh * MLA_WP + LANE:(hh + 1) * MLA_WP] = _rope(qr[:, blk], cm, s1m, s2m, MLA_ROPE // 2).astype(BF16)

    ckvn = _rms(seg(C_CKV, C_KR), kvn_ref[...])
    kr = _rope(seg(C_KR, C_QB), cm, s1m, s2m, MLA_ROPE // 2)
    mlaf_ref[:, 0:MLA_KV_LORA] = ckvn
    mlaf_ref[:, MLA_KV_LORA:MLA_W] = kr[:, 0:MLA_ROPE]
    mlab_ref[:, 0:LANE] = ckvn.astype(BF16)
    mlab_ref[:, LANE:MLA_WP] = kr.astype(BF16)

    qb = seg(C_QB, C_CMP)
    for hh in range(N_HEADS):
        blk = slice(hh * LANE, (hh + 1) * LANE)
        qnsa_ref[:, blk] = _rope(qb[:, blk], cd, s1d, s2d, ROT_HALF).astype(BF16)
    cmpf_ref[...] = _rope(seg(C_CMP, C_SEL), cs, s1s, s2s, ROT_HALF)
    sl = _rope(seg(C_SEL, C_WIN), cs, s1s, s2s, ROT_HALF)
    self_ref[...] = sl
    selb_ref[...] = sl.astype(BF16)
    wn = _rope(seg(C_WIN, C_GB), cs, s1s, s2s, ROT_HALF)
    winf_ref[...] = wn
    winb_ref[...] = wn.astype(BF16)
    gate_ref[...] = jax.nn.sigmoid(seg(C_GB, C_QC))

    qc = seg(C_QC, C_KVM)
    for hh in range(N_HEADS):
        blk = slice(hh * LANE, (hh + 1) * LANE)
        qmoba_ref[:, blk] = _rope(qc[:, blk], cd, s1d, s2d, ROT_HALF).astype(BF16)
    kvm = seg(C_KVM, C_END)
    km = _rope(kvm[:, 0:LANE], cd, s1d, s2d, ROT_HALF)
    mobaf_ref[:, 0:LANE] = km
    mobaf_ref[:, LANE:2 * LANE] = kvm[:, LANE:2 * LANE]
    mobab_ref[:, 0:LANE] = km.astype(BF16)
    mobab_ref[:, LANE:2 * LANE] = kvm[:, LANE:2 * LANE].astype(BF16)


def _project(x, g, wl, tabs, tm):
    t, d = x.shape
    row = lambda w: pl.BlockSpec((tm, w), lambda i: (i, 0))
    full = lambda a: pl.BlockSpec(a.shape, lambda i: (0,) * a.ndim)
    consts = (g, wl['wp'], wl['qn'], wl['kvn'], wl['wuqn'], wl['wuqr'], wl['wuk'])
    outs = [(N_HEADS * MLA_WP, BF16), (MLA_W, F32), (MLA_WP, BF16), (N_HEADS * LANE, BF16), (LANE, F32),
            (LANE, F32), (LANE, BF16), (LANE, F32), (LANE, BF16), (LANE, F32), (N_HEADS * LANE, BF16),
            (2 * LANE, F32), (2 * LANE, BF16)]
    return pl.pallas_call(
        _proj_kernel,
        grid=(t // tm,),
        in_specs=[row(d)] + [full(a) for a in consts] + [row(LANE)] * 6,
        out_specs=[row(w) for w, _ in outs],
        out_shape=[jax.ShapeDtypeStruct((t, w), dt) for w, dt in outs],
        compiler_params=_params(("parallel",)),
        name="proj",
    )(x, *consts, *tabs)


def _attn_schedule(seq, tq, tk, mode):
    steps = []
    for i in range(seq // tq):
        hi = (i * tq + tq - 1) // tk
        lo = max(i * tq - NSA_WINDOW, 0) // tk if mode == 'win' else 0
        steps += [(i, kb, int(kb == lo), int(kb == hi)) for kb in range(lo, hi + 1)]
    return np.asarray(steps, np.int32).T


def _attn_kernel(sched_ref, *refs, mode, tq, tk, kw, voff, scale, hg):
    if mode in ('sel', 'moba'):
        q_ref, kv_ref, sel_ref, o_ref, qs_ref, m_ref, l_ref, acc_ref, sels_ref = refs
    else:
        q_ref, kv_ref, o_ref, qs_ref, m_ref, l_ref, acc_ref = refs
    t = pl.program_id(1)
    i, kb = sched_ref[0, t], sched_ref[1, t]

    @pl.when(sched_ref[2, t] == 1)
    def _():
        for hh in range(N_HEADS):
            qs_ref[hh * tq:(hh + 1) * tq, :] = q_ref[:, hh * kw:(hh + 1) * kw]
            if mode == 'moba':
                sels_ref[hh * tq:(hh + 1) * tq, :] = sel_ref[:, hh * LANE:(hh + 1) * LANE]
        m_ref[...] = jnp.full(m_ref.shape, MASK_FLOOR, F32)
        l_ref[...] = jnp.zeros(l_ref.shape, F32)
        acc_ref[...] = jnp.zeros(acc_ref.shape, F32)

    kpos = kb * tk + lax.broadcasted_iota(jnp.int32, (1, tk), 1)
    qpos = i * tq + lax.broadcasted_iota(jnp.int32, (tq, 1), 0)
    causal = kpos <= qpos
    if mode == 'mla':
        mask2 = causal
    elif mode == 'win':
        mask2 = causal & (qpos - kpos <= NSA_WINDOW)
    else:
        shift = 6 if mode == 'sel' else 8
        expand = jnp.where((kpos >> shift) == lax.broadcasted_iota(jnp.int32, (LANE, 1), 0), 1.0, 0.0).astype(BF16)
        if mode == 'sel':
            mask2 = (_dot(sel_ref[...], expand) > 0.5) & causal
        else:
            mask2 = causal & ((kpos >> shift) == (qpos >> shift))
    bias = jnp.where(mask2, 1.0, 0.0) if mode == 'moba' else jnp.where(mask2, 0.0, NEG)
    for g in range(N_HEADS // hg):
        hs = slice(g * hg, (g + 1) * hg)
        rs = slice(g * hg * tq, (g + 1) * hg * tq)
        s3 = (_dot_nt(qs_ref[rs, :], kv_ref[:, 0:kw]) * scale).reshape(hg, tq, tk)
        if mode == 'moba':
            vis = jnp.maximum(_dot(sels_ref[rs, :], expand).reshape(hg, tq, tk), bias[None])
            s3 = s3 + (vis - 1.0) * (-NEG)
        else:
            s3 = s3 + bias[None]
        m_old = m_ref[hs]
        m_new = jnp.maximum(m_old, jnp.max(s3, axis=-1, keepdims=True))
        p3 = jnp.exp(s3 - m_new)
        alpha = jnp.exp(m_old - m_new)
        l_ref[hs] = alpha * l_ref[hs] + jnp.sum(p3, axis=-1, keepdims=True)
        pv = _dot(p3.reshape(hg * tq, tk).astype(BF16), kv_ref[:, voff:voff + LANE])
        acc_ref[hs] = alpha * acc_ref[hs] + pv.reshape(hg, tq, LANE)
        m_ref[hs] = m_new

    @pl.when(sched_ref[3, t] == 1)
    def _():
        o = acc_ref[...] / jnp.maximum(l_ref[...], 1e-30)
        for hh in range(N_HEADS):
            o_ref[:, hh * LANE:(hh + 1) * LANE] = o[hh]


def _prompt_attention(q, kv, sel, *, mode, n_seq, seq, tq=128, tk=512, hg=2):
    tk = min(tk, seq)
    kw = q.shape[1] // N_HEADS
    wkv = kv.shape[1]
    voff = LANE if mode == 'moba' else 0
    scale = MLA_SCALE if mode == 'mla' else ATTN_SCALE
    nq, nkb = seq // tq, seq // tk
    sched = _attn_schedule(seq, tq, tk, mode)
    q_map = lambda b, t, sc: (b * nq + sc[0, t], 0)
    kv_map = lambda b, t, sc: (b * nkb + sc[1, t], 0)
    in_specs = [pl.BlockSpec((tq, N_HEADS * kw), q_map), pl.BlockSpec((tk, wkv), kv_map)]
    args = [q, kv]
    scratch = [pltpu.VMEM((N_HEADS * tq, kw), BF16), pltpu.VMEM((N_HEADS, tq, 1), F32),
               pltpu.VMEM((N_HEADS, tq, 1), F32), pltpu.VMEM((N_HEADS, tq, LANE), F32)]
    if mode in ('sel', 'moba'):
        in_specs.append(pl.BlockSpec((tq, sel.shape[1]), q_map))
        args.append(sel)
        scratch.append(pltpu.VMEM((N_HEADS * tq, LANE), BF16))
    return pl.pallas_call(
        functools.partial(_attn_kernel, mode=mode, tq=tq, tk=tk, kw=kw, voff=voff, scale=scale, hg=hg),
        grid_spec=pltpu.PrefetchScalarGridSpec(
            num_scalar_prefetch=1, grid=(n_seq, sched.shape[1]), in_specs=in_specs,
            out_specs=pl.BlockSpec((tq, N_HEADS * LANE), q_map), scratch_shapes=scratch),
        out_shape=jax.ShapeDtypeStruct((n_seq * seq, N_HEADS * LANE), F32),
        compiler_params=_params(("parallel", "arbitrary")),
        name="attn_" + mode,
    )(jnp.asarray(sched), *args)


def _compress(zla, zlb, w1a_ref, w1b_ref, w2_ref):
    a = _dot(zla, w1a_ref[...])
    b = _dot(zlb, w1b_ref[...])
    hid = jax.nn.gelu(a + pltpu.roll(b, b.shape[0] - 1, 0))
    return _dot(hid.astype(BF16), w2_ref[...])


def _cmp_attend(qs, tok, qpos, n_cmp, head_axis):
    ncp = tok.shape[0]
    rows = qs.shape[0]
    nq = rows // N_HEADS
    shape3 = (N_HEADS, nq, ncp) if head_axis == 0 else (nq, N_HEADS, ncp)
    s3 = (_dot_nt(qs, tok) * ATTN_SCALE).reshape(shape3)
    nidx = lax.broadcasted_iota(jnp.int32, (1, 1, ncp), 2)
    mask = ((nidx * CMP_STRIDE + (CMP_BLOCK - 1)) <= qpos) & (nidx < n_cmp)
    s3 = jnp.where(mask, s3, NEG)
    e = jnp.where(mask, jnp.exp(s3 - jnp.max(s3, axis=-1, keepdims=True)), 0.0)
    p3 = e / jnp.maximum(jnp.sum(e, axis=-1, keepdims=True), 1e-30)
    o = _dot(p3.reshape(rows, ncp).astype(BF16), tok)
    return o, jnp.sum(p3, axis=head_axis)


def _select_blocks(imp, mslc_ref, cur):
    pslc = jnp.dot(imp, mslc_ref[...], precision=lax.Precision.HIGHEST, preferred_element_type=F32)
    jb = lax.broadcasted_iota(jnp.int32, (1, pslc.shape[1]), 1)
    forced = (jb == 0) | (jb == cur) | (jb == cur - 1)
    score = jnp.where(jb <= cur, pslc + NSA_FORCE * forced.astype(F32), NEG)
    sel = _topk_mask(score, SEL_TOPK, jb.astype(F32))
    return jnp.where(jb <= cur, sel, 0.0)


def _nsa_select_kernel(cmp_ref, q_ref, pa_ref, pb_ref, w1a_ref, w1b_ref, w2_ref, mslc_ref,
                       ocmp_ref, sel_ref, zl_ref, tok_ref, *, tq, n_cmp):
    i = pl.program_id(1)
    nch = zl_ref.shape[0]

    @pl.when(i == 0)
    def _():
        for b in range(CMP_STRIDE):
            zl_ref[:, b * LANE:(b + 1) * LANE] = cmp_ref[pl.ds(b, nch, stride=CMP_STRIDE), :]
        zl = zl_ref[...]
        tok = _compress((zl + pa_ref[...]).astype(BF16), (zl + pb_ref[...]).astype(BF16), w1a_ref, w1b_ref, w2_ref)
        tok_ref[...] = tok.astype(BF16)

    qs = jnp.concatenate([q_ref[:, hh * LANE:(hh + 1) * LANE] for hh in range(N_HEADS)], axis=0)
    qpos = i * tq + lax.broadcasted_iota(jnp.int32, (1, tq, 1), 1)
    o, imp = _cmp_attend(qs, tok_ref[...], qpos, n_cmp, 0)
    for hh in range(N_HEADS):
        ocmp_ref[:, hh * LANE:(hh + 1) * LANE] = o[hh * tq:(hh + 1) * tq]
    sel_ref[...] = _select_blocks(imp, mslc_ref, qpos[0] >> 6).astype(BF16)


def _slc_matrix(n_rows, n_cmp, n_cols, n_sel):
    a, r = SEL_BLOCK // CMP_STRIDE, CMP_BLOCK // CMP_STRIDE
    n = np.arange(n_rows)[:, None]
    j = np.arange(n_cols)[None, :]
    ok = (n >= a * j - (r - 1)) & (n <= a * j + a - 1) & (n < n_cmp) & (j < n_sel)
    return jnp.asarray(ok.astype(np.float32))


def _nsa_select_prompt(cmpf, qnsa, wl, *, n_seq, seq, tq=128):
    nch = seq // CMP_STRIDE
    n_cmp = (seq - CMP_BLOCK) // CMP_STRIDE + 1
    n_sel = max(-(-seq // SEL_BLOCK), SEL_TOPK)
    assert n_sel <= LANE and n_cmp + 1 == nch
    mslc = _slc_matrix(nch, n_cmp, LANE, n_sel)
    nq = seq // tq
    consts = (wl['pa'], wl['pb'], wl['w1a'], wl['w1b'], wl['w2'], mslc)
    full = lambda a: pl.BlockSpec(a.shape, lambda b, i: (0,) * a.ndim)
    q_map = lambda b, i: (b * nq + i, 0)
    return pl.pallas_call(
        functools.partial(_nsa_select_kernel, tq=tq, n_cmp=n_cmp),
        grid=(n_seq, nq),
        in_specs=[pl.BlockSpec((seq, LANE), lambda b, i: (b, 0)), pl.BlockSpec((tq, N_HEADS * LANE), q_map)]
        + [full(a) for a in consts],
        out_specs=[pl.BlockSpec((tq, N_HEADS * LANE), q_map), pl.BlockSpec((tq, LANE), q_map)],
        out_shape=[jax.ShapeDtypeStruct((n_seq * seq, N_HEADS * LANE), F32),
                   jax.ShapeDtypeStruct((n_seq * seq, LANE), BF16)],
        scratch_shapes=[pltpu.VMEM((nch, CMP_STRIDE * LANE), F32), pltpu.VMEM((nch, LANE), BF16)],
        compiler_params=_params(("parallel", "arbitrary")),
        name="nsa_select",
    )(cmpf, qnsa, *consts)


def _moba_select_kernel(rows_ref, q_ref, sel_ref, means_ref, *, tq, n_blk):
    i = pl.program_id(1)

    @pl.when(i == 0)
    def _():
        k = rows_ref[:, 0:LANE]
        means = jnp.sum(k.reshape(n_blk, MOBA_BLOCK, LANE), axis=1) * (1.0 / MOBA_BLOCK)
        means_ref[...] = jnp.zeros(means_ref.shape, BF16)
        means_ref[0:n_blk, :] = means.astype(BF16)

    qs = jnp.concatenate([q_ref[:, hh * LANE:(hh + 1) * LANE] for hh in range(N_HEADS)], axis=0)
    gsc = _dot_nt(qs, means_ref[...])
    rows = N_HEADS * tq
    cur = (i * tq + (lax.broadcasted_iota(jnp.int32, (rows, 1), 0) & (tq - 1))) >> 8
    jb = lax.broadcasted_iota(jnp.int32, (1, LANE), 1)
    sel = _topk_mask(jnp.where(jb < cur, gsc, NEG), MOBA_TOPK, jb.astype(F32))
    sel = jnp.where(jb < cur, sel, 0.0).astype(BF16)
    for hh in range(N_HEADS):
        sel_ref[:, hh * LANE:(hh + 1) * LANE] = sel[hh * tq:(hh + 1) * tq]


def _moba_select_prompt(mobaf, qmoba, *, n_seq, seq, tq=128):
    n_blk = seq // MOBA_BLOCK
    assert seq % MOBA_BLOCK == 0 and MOBA_TOPK <= n_blk <= 16
    nq = seq // tq
    q_map = lambda b, i: (b * nq + i, 0)
    return pl.pallas_call(
        functools.partial(_moba_select_kernel, tq=tq, n_blk=n_blk),
        grid=(n_seq, nq),
        in_specs=[pl.BlockSpec((seq, 2 * LANE), lambda b, i: (b, 0)), pl.BlockSpec((tq, N_HEADS * LANE), q_map)],
        out_specs=pl.BlockSpec((tq, N_HEADS * LANE), q_map),
        out_shape=jax.ShapeDtypeStruct((n_seq * seq, N_HEADS * LANE), BF16),
        scratch_shapes=[pltpu.VMEM((LANE, LANE), BF16)],
        compiler_params=_params(("parallel", "arbitrary")),
        name="moba_select",
    )(mobaf, qmoba)


def _softmax2(sp, sn, mn):
    sn = jnp.where(mn, sn, NEG)
    m = jnp.maximum(jnp.max(sp, axis=-1, keepdims=True), jnp.max(sn, axis=-1, keepdims=True))
    ep = jnp.exp(sp - m)
    en = jnp.where(mn, jnp.exp(sn - m), 0.0)
    den = jnp.sum(ep, axis=-1, keepdims=True) + jnp.sum(en, axis=-1, keepdims=True)
    return ep, en, den


def _new_mask(n_new_pad):
    qi = lax.broadcasted_iota(jnp.int32, (4 * N_HEADS, 1), 0) >> 3
    kj = lax.broadcasted_iota(jnp.int32, (1, n_new_pad), 1)
    return kj <= qi


def _mla_sample_kernel(pt_ref, qs_ref, new_ref, *rest, group, n_steps, chain):
    pages, (o_ref, m_ref, l_ref, acc_ref) = rest[:group], rest[group:]
    s = pl.program_id(1)

    @pl.when(s == 0)
    def _():
        m_ref[...] = jnp.full(m_ref.shape, MASK_FLOOR, F32)
        l_ref[...] = jnp.zeros(l_ref.shape, F32)
        acc_ref[...] = jnp.zeros(acc_ref.shape, F32)

    qs = qs_ref[...]
    zpad = jnp.zeros((MLA_WP - MLA_W, chain * PAGE), BF16)
    for c in range(group // chain):
        kt = jnp.concatenate([pages[c * chain + g][...].astype(BF16) for g in range(chain)], axis=1)
        sc = _dot(qs, jnp.concatenate([kt, zpad], axis=0)) * MLA_SCALE
        m_old = m_ref[c]
        m_new = jnp.maximum(m_old, jnp.max(sc, axis=-1, keepdims=True))
        p = jnp.exp(sc - m_new)
        alpha = jnp.exp(m_old - m_new)
        l_ref[c] = alpha * l_ref[c] + jnp.sum(p, axis=-1, keepdims=True)
        acc_ref[c] = alpha * acc_ref[c] + _dot_nt(p.astype(BF16), kt[0:LANE, :])
        m_ref[c] = m_new

    @pl.when(s == n_steps - 1)
    def _():
        new = new_ref[...]
        mn = _new_mask(new.shape[0])
        sn = jnp.where(mn, _dot_nt(qs, new) * MLA_SCALE, NEG)
        ms = m_ref[...]
        m_fin = jnp.maximum(jnp.max(ms, axis=0), jnp.max(sn, axis=-1, keepdims=True))
        en = jnp.where(mn, jnp.exp(sn - m_fin), 0.0)
        w = jnp.exp(ms - m_fin[None])
        den = jnp.sum(w * l_ref[...], axis=0) + jnp.sum(en, axis=-1, keepdims=True)
        o_ref[...] = (jnp.sum(w * acc_ref[...], axis=0) + _dot(en.astype(BF16), new[:, 0:LANE])) / den


def _page_specs(layer, n_pages, group, feat):
    def spec(g):
        return pl.BlockSpec((None, None, feat, PAGE),
                            lambda b, s, pt: (layer, pt[b * n_pages + s * group + g], 0, 0))
    return [spec(g) for g in range(group)]


def _mla_sample(pt, qs, new, cache_t, layer, *, group=64, chain=16):
    n_seq, n_pages = qs.shape[0], pt.shape[0] // qs.shape[0]
    group = min(group, n_pages)
    n_steps = n_pages // group
    nc = group // chain
    rows = 4 * N_HEADS
    per_seq = lambda a: pl.BlockSpec((None,) + a.shape[1:], lambda b, s, pt: (b, 0, 0))
    return pl.pallas_call(
        functools.partial(_mla_sample_kernel, group=group, n_steps=n_steps, chain=chain),
        grid_spec=pltpu.PrefetchScalarGridSpec(
            num_scalar_prefetch=1, grid=(n_seq, n_steps),
            in_specs=[per_seq(qs), per_seq(new)] + _page_specs(layer, n_pages, group, MLA_W),
            out_specs=pl.BlockSpec((None, rows, LANE), lambda b, s, pt: (b, 0, 0)),
            scratch_shapes=[pltpu.VMEM((nc, rows, 1), F32), pltpu.VMEM((nc, rows, 1), F32),
                            pltpu.VMEM((nc, rows, LANE), F32)]),
        out_shape=jax.ShapeDtypeStruct((n_seq, rows, LANE), F32),
        compiler_params=_params(("arbitrary", "arbitrary")),
        name="mla_sample",
    )(pt, qs, new, *([cache_t] * group))


def _moba_sample_kernel(pt_ref, qs_ref, new_ref, *rest, group, n_steps, n_blk):
    pages, (o_ref, gblk_ref, mblk_ref, lblk_ref, oblk_ref) = rest[:group], rest[group:]
    s = pl.program_id(1)
    lane = lax.broadcasted_iota(jnp.int32, (1, LANE), 1)
    ppb = MOBA_BLOCK // PAGE
    bps = group // ppb

    @pl.when(s == 0)
    def _():
        gblk_ref[...] = jnp.zeros(gblk_ref.shape, F32)
        mblk_ref[...] = jnp.full(mblk_ref.shape, NEG, F32)
        lblk_ref[...] = jnp.zeros(lblk_ref.shape, F32)

    qs = qs_ref[...]
    gacc, macc, lacc = gblk_ref[...], mblk_ref[...], lblk_ref[...]
    for j in range(bps):
        blk = s * bps + j
        xs = [pages[j * ppb + t][...] for t in range(ppb)]
        kt = jnp.concatenate([x[0:LANE, :].astype(BF16) for x in xs], axis=1)
        vt = jnp.concatenate([x[LANE:2 * LANE, :].astype(BF16) for x in xs], axis=1)
        sc = _dot(qs, kt) * ATTN_SCALE
        mb = jnp.max(sc, axis=-1, keepdims=True)
        p = jnp.exp(sc - mb)
        here = lane == blk
        gacc = jnp.where(here, jnp.sum(sc, axis=-1, keepdims=True) * (1.0 / (MOBA_BLOCK * ATTN_SCALE)), gacc)
        macc = jnp.where(here, mb, macc)
        lacc = jnp.where(here, jnp.sum(p, axis=-1, keepdims=True), lacc)
        oblk_ref[blk] = _dot_nt(p.astype(BF16), vt)
    gblk_ref[...], mblk_ref[...], lblk_ref[...] = gacc, macc, lacc

    @pl.when(s == n_steps - 1)
    def _():
        new = new_ref[...]
        ok = lane < n_blk
        sel = _topk_mask(jnp.where(ok, gblk_ref[...], NEG), MOBA_TOPK, lane.astype(F32))
        pick = ok & (sel > 0.5)
        mn = _new_mask(new.shape[0])
        sn = jnp.where(mn, _dot_nt(qs, new[:, 0:LANE]) * ATTN_SCALE, NEG)
        mblk = mblk_ref[...]
        m_fin = jnp.maximum(jnp.max(jnp.where(pick, mblk, NEG), axis=-1, keepdims=True),
                            jnp.max(sn, axis=-1, keepdims=True))
        w = jnp.where(pick, jnp.exp(mblk - m_fin), 0.0)
        en = jnp.where(mn, jnp.exp(sn - m_fin), 0.0)
        den = jnp.sum(w * lblk_ref[...], axis=-1, keepdims=True) + jnp.sum(en, axis=-1, keepdims=True)
        o = _dot(en.astype(BF16), new[:, LANE:2 * LANE])
        for b in range(n_blk):
            o = o + w[:, b:b + 1] * oblk_ref[b]
        o_ref[...] = o / den


def _moba_sample(pt, qs, new, cache_t, layer, *, group=64):
    n_seq, n_pages = qs.shape[0], pt.shape[0] // qs.shape[0]
    n_blk = n_pages * PAGE // MOBA_BLOCK
    assert n_blk <= LANE and n_pages % 2 == 0
    group = min(group, n_pages)
    n_steps = n_pages // group
    rows = 4 * N_HEADS
    per_seq = lambda a: pl.BlockSpec((None,) + a.shape[1:], lambda b, s, pt: (b, 0, 0))
    return pl.pallas_call(
        functools.partial(_moba_sample_kernel, group=group, n_steps=n_steps, n_blk=n_blk),
        grid_spec=pltpu.PrefetchScalarGridSpec(
            num_scalar_prefetch=1, grid=(n_seq, n_steps),
            in_specs=[per_seq(qs), per_seq(new)] + _page_specs(layer, n_pages, group, 2 * LANE),
            out_specs=pl.BlockSpec((None, rows, LANE), lambda b, s, pt: (b, 0, 0)),
            scratch_shapes=[pltpu.VMEM((rows, LANE), F32), pltpu.VMEM((rows, LANE), F32),
                            pltpu.VMEM((rows, LANE), F32), pltpu.VMEM((n_blk, rows, LANE), F32)]),
        out_shape=jax.ShapeDtypeStruct((n_seq, rows, LANE), F32),
        compiler_params=_params(("arbitrary", "arbitrary")),
        name="moba_sample",
    )(pt, qs, new, *([cache_t] * group))


def _nsa_sample_kernel(pt_ref, qs_ref, newsel_ref, newwin_ref, win_ref, gate_ref, pat_ref, pbt_ref, perm_ref,
                       w1a_ref, w1b_ref, w2_ref, mslc_ref, shift_ref, expand_ref, *rest,
                       group, n_steps, n_cmp, past_len):
    cmp_pages, sel_pages = rest[:group], rest[group:2 * group]
    o_ref, zla_ref, zlb_ref, ocmp_ref, selm_ref, m_ref, l_ref, acc_ref = rest[2 * group:]
    s = pl.program_id(1)
    qs = qs_ref[...]
    nrow = qs.shape[0]

    @pl.when(s < n_steps)
    def _():
        perm = perm_ref[...]
        for g in range(0, group, 2):
            x = jnp.concatenate([cmp_pages[g][...], cmp_pages[g + 1][...]], axis=1)
            ra = _dot_nt(perm, (x + pat_ref[...]).astype(BF16)).astype(BF16)
            rb = _dot_nt(perm, (x + pbt_ref[...]).astype(BF16)).astype(BF16)
            r0 = pl.multiple_of((s * group + g) * 8, 16)
            for bb in range(CMP_STRIDE):
                zla_ref[pl.ds(r0, 16), bb * LANE:(bb + 1) * LANE] = ra[bb * 16:(bb + 1) * 16, :]
                zlb_ref[pl.ds(r0, 16), bb * LANE:(bb + 1) * LANE] = rb[bb * 16:(bb + 1) * 16, :]

    @pl.when(s == n_steps - 1)
    def _():
        qidx = lax.broadcasted_iota(jnp.int32, (nrow // N_HEADS, 1, 1), 0)
        tok = _compress(zla_ref[...], zlb_ref[...], w1a_ref, w1b_ref, w2_ref).astype(BF16)
        o_cmp, imp = _cmp_attend(qs, tok, past_len + qidx, n_cmp, 1)
        imp = jnp.broadcast_to(imp[:, None, :], (nrow // N_HEADS, N_HEADS, imp.shape[1])).reshape(nrow, imp.shape[1])
        cur = (past_len + (lax.broadcasted_iota(jnp.int32, (nrow, 1), 0) >> 3)) >> 6
        ocmp_ref[...] = o_cmp
        selm_ref[...] = _select_blocks(imp, mslc_ref, cur).astype(BF16)
        m_ref[...] = jnp.full(m_ref.shape, MASK_FLOOR, F32)
        l_ref[...] = jnp.zeros(l_ref.shape, F32)
        acc_ref[...] = jnp.zeros(acc_ref.shape, F32)

    @pl.when(s >= n_steps)
    def _():
        st = jnp.concatenate([sel_pages[g][...].astype(BF16) for g in range(group)], axis=1)
        picked = _dot(_dot(selm_ref[...], shift_ref[s - n_steps]).astype(BF16), expand_ref[...])
        sc = _dot(qs, st) * ATTN_SCALE + (picked - 1.0) * (-NEG)
        m_old = m_ref[...]
        m_new = jnp.maximum(m_old, jnp.max(sc, axis=-1, keepdims=True))
        p = jnp.exp(sc - m_new)
        alpha = jnp.exp(m_old - m_new)
        l_ref[...] = alpha * l_ref[...] + jnp.sum(p, axis=-1, keepdims=True)
        acc_ref[...] = alpha * acc_ref[...] + _dot_nt(p.astype(BF16), st)
        m_ref[...] = m_new

    @pl.when(s == 2 * n_steps - 1)
    def _():
        mn = _new_mask(newsel_ref.shape[0])
        new = newsel_ref[...]
        sn = jnp.where(mn, _dot_nt(qs, new) * ATTN_SCALE, NEG)
        m_fin = jnp.maximum(m_ref[...], jnp.max(sn, axis=-1, keepdims=True))
        en = jnp.where(mn, jnp.exp(sn - m_fin), 0.0)
        w = jnp.exp(m_ref[...] - m_fin)
        den = w * l_ref[...] + jnp.sum(en, axis=-1, keepdims=True)
        o_sel = (w * acc_ref[...] + _dot(en.astype(BF16), new)) / den
        o_cmp = ocmp_ref[...]
        wt = win_ref[...].astype(BF16)
        neww = newwin_ref[...]
        sw = _dot(qs, wt) * ATTN_SCALE
        vis = lax.broadcasted_iota(jnp.int32, (1, wt.shape[1]), 1) >= (lax.broadcasted_iota(jnp.int32, (nrow, 1), 0) >> 3)
        sw = jnp.where(vis, sw, NEG)
        ew, enw, denw = _softmax2(sw, _dot_nt(qs, neww) * ATTN_SCALE, mn)
        ew = jnp.where(vis, ew, 0.0)
        o_win = (_dot_nt(ew.astype(BF16), wt) + _dot(enw.astype(BF16), neww)) / denw
        gt = gate_ref[...]
        o_ref[...] = gt[:, 0:1] * o_cmp + gt[:, 1:2] * o_sel + gt[:, 2:3] * o_win


def _nsa_sample(pt, qs, newsel, newwin, win_t, gates, cmp_t, sel_t, wl, layer, *, group=64):
    n_seq, n_pages = qs.shape[0], pt.shape[0] // qs.shape[0]
    past_len = n_pages * PAGE
    group = min(group, n_pages)
    n_steps = n_pages // group
    nch = past_len // CMP_STRIDE
    length = past_len + 4
    n_cmp = (length - CMP_BLOCK) // CMP_STRIDE + 1
    n_sel = max(-(-length // SEL_BLOCK), SEL_TOPK)
    assert n_cmp + 1 == nch
    nsp = -(-n_sel // LANE) * LANE
    mslc = _slc_matrix(nch, n_cmp, nsp, n_sel)
    chunk = group * PAGE
    n_chunks = n_steps
    bpc = chunk // SEL_BLOCK
    shift = np.stack([(np.arange(nsp)[:, None] == c * bpc + np.arange(LANE)[None, :]) & (np.arange(LANE)[None, :] < bpc)
                      for c in range(n_chunks)]).astype(np.float32)
    expand = (np.arange(LANE)[:, None] == (np.arange(chunk)[None, :] // SEL_BLOCK)).astype(np.float32)
    r = np.arange(2 * PAGE)
    perm = (r[None, :] == (CMP_STRIDE * (r[:, None] % 16) + r[:, None] // 16)).astype(np.float32)
    shift, expand, perm = jnp.asarray(shift, BF16), jnp.asarray(expand, BF16), jnp.asarray(perm, BF16)
    consts = (wl['pat'], wl['pbt'], perm, wl['w1a'], wl['w1b'], wl['w2'], mslc, shift, expand)
    per_seq = lambda a: pl.BlockSpec((None,) + a.shape[1:], lambda b, s, pt: (b, 0, 0))
    full = lambda a: pl.BlockSpec(a.shape, lambda b, s, pt: (0,) * a.ndim)
    win_spec = pl.BlockSpec((None, None, LANE, win_t.shape[3]), lambda b, s, pt: (layer, b, 0, 0))
    rows = 4 * N_HEADS

    def page_specs(phase):
        def spec(g):
            def index(b, s, pt):
                step = jnp.minimum(s, n_steps - 1) if phase == 0 else jnp.maximum(s - n_steps, 0)
                return (layer, pt[b * n_pages + step * group + g], 0, 0)
            return pl.BlockSpec((None, None, LANE, PAGE), index)
        return [spec(g) for g in range(group)]

    return pl.pallas_call(
        functools.partial(_nsa_sample_kernel, group=group, n_steps=n_steps, n_cmp=n_cmp, past_len=past_len),
        grid_spec=pltpu.PrefetchScalarGridSpec(
            num_scalar_prefetch=1, grid=(n_seq, 2 * n_steps),
            in_specs=[per_seq(qs), per_seq(newsel), per_seq(newwin), win_spec, per_seq(gates)]
            + [full(a) for a in consts] + page_specs(0) + page_specs(1),
            out_specs=pl.BlockSpec((None, rows, LANE), lambda b, s, pt: (b, 0, 0)),
            scratch_shapes=[pltpu.VMEM((nch, CMP_STRIDE * LANE), BF16), pltpu.VMEM((nch, CMP_STRIDE * LANE), BF16),
                            pltpu.VMEM((rows, LANE), F32), pltpu.VMEM((rows, nsp), BF16),
                            pltpu.VMEM((rows, 1), F32), pltpu.VMEM((rows, 1), F32), pltpu.VMEM((rows, LANE), F32)]),
        out_shape=jax.ShapeDtypeStruct((n_seq, rows, LANE), F32),
        compiler_params=_params(("arbitrary", "arbitrary")),
        name="nsa_sample",
    )(pt, qs, newsel, newwin, win_t, gates, *consts, *([cmp_t] * group), *([sel_t] * group))


def _merge_kernel(x_ref, g_ref, olat_ref, ocmp_ref, osel_ref, owin_ref, gate_ref, omoba_ref,
                  wgm_ref, wuv_ref, wb0_ref, wb1_ref, wb2_ref, wout_ref, o_ref):
    x = x_ref[...]
    d = x.shape[1]
    h = _rms(x, g_ref[...]).astype(BF16)
    gm = jax.nn.sigmoid(_dot(h, wgm_ref[...]))
    up0 = _dot(_dot(olat_ref[...].astype(BF16), wuv_ref[...]).astype(BF16), wb0_ref[...])
    gt = gate_ref[...]
    parts = []
    for hh in range(N_HEADS):
        blk = slice(hh * LANE, (hh + 1) * LANE)
        parts.append(gt[:, hh:hh + 1] * ocmp_ref[:, blk] + gt[:, N_HEADS + hh:N_HEADS + hh + 1] * osel_ref[:, blk]
                     + gt[:, 2 * N_HEADS + hh:2 * N_HEADS + hh + 1] * owin_ref[:, blk])
    up1 = _dot(jnp.concatenate(parts, axis=1).astype(BF16), wb1_ref[...])
    up2 = _dot(omoba_ref[...].astype(BF16), wb2_ref[...])
    mix = gm[:, 0:d] * up0 + gm[:, d:2 * d] * up1 + gm[:, 2 * d:3 * d] * up2
    o_ref[...] = x + _dot(mix.astype(BF16), wout_ref[...])


def _merge(x, g, olat, ocmp, osel, owin, gates, omoba, wl, tm):
    t, d = x.shape
    row = lambda a: pl.BlockSpec((tm, a.shape[1]), lambda i: (i, 0))
    full = lambda a: pl.BlockSpec(a.shape, lambda i: (0,) * a.ndim)
    acts = (olat, ocmp, osel, owin, gates, omoba)
    consts = (wl['wgm'], wl['wuv'], wl['wb0'], wl['wb1'], wl['wb2'], wl['wout'])
    return pl.pallas_call(
        _merge_kernel,
        grid=(t // tm,),
        in_specs=[row(x), full(g)] + [row(a) for a in acts] + [full(a) for a in consts],
        out_specs=row(x),
        out_shape=jax.ShapeDtypeStruct((t, d), F32),
        compiler_params=_params(("parallel",)),
        name="merge",
    )(x, g, *acts, *consts)


def _ffn_kernel(x_ref, g_ref, gf_ref, wg_ref, wu_ref, wd_ref, o_ref, h_ref, acc_ref, *, nf, final):
    f = pl.program_id(1)

    @pl.when(f == 0)
    def _():
        h_ref[...] = _rms(x_ref[...], g_ref[...]).astype(BF16)
        acc_ref[...] = jnp.zeros(acc_ref.shape, F32)

    h = h_ref[...]
    act = jax.nn.silu(_dot(h, wg_ref[...])) * _dot(h, wu_ref[...])
    acc_ref[...] += _dot(act.astype(BF16), wd_ref[...])

    @pl.when(f == nf - 1)
    def _():
        y = x_ref[...] + acc_ref[...]
        o_ref[...] = _rms(y, gf_ref[...]) if final else y


def _ffn(x, g, gfinal, wgu, wd, *, final, tm=512, tf=512):
    t, d = x.shape
    tm = tm if t % tm == 0 else 256
    dff = wd.shape[0]
    nf = dff // tf
    return pl.pallas_call(
        functools.partial(_ffn_kernel, nf=nf, final=final),
        grid=(t // tm, nf),
        in_specs=[pl.BlockSpec((tm, d), lambda i, f: (i, 0)), pl.BlockSpec((1, d), lambda i, f: (0, 0)),
                  pl.BlockSpec((1, d), lambda i, f: (0, 0)),
                  pl.BlockSpec((d, tf), lambda i, f: (0, f)), pl.BlockSpec((d, tf), lambda i, f: (0, nf + f)),
                  pl.BlockSpec((tf, d), lambda i, f: (f, 0))],
        out_specs=pl.BlockSpec((tm, d), lambda i, f: (i, 0)),
        out_shape=jax.ShapeDtypeStruct((t, d), F32),
        scratch_shapes=[pltpu.VMEM((tm, d), BF16), pltpu.VMEM((tm, d), F32)],
        compiler_params=_params(("parallel", "arbitrary")),
        name="ffn",
    )(x, g, gfinal, wgu, wgu, wd)


def _moe_kernel(x_ref, g_ref, gf_ref, wr_ref, wg_ref, wu_ref, wd_ref, o_ref, h_ref, gate_ref, acc_ref,
                *, ne, nf, final):
    e, f = pl.program_id(1), pl.program_id(2)
    lane = lax.broadcasted_iota(jnp.int32, (1, LANE), 1)

    @pl.when((e == 0) & (f == 0))
    def _():
        h = _rms(x_ref[...], g_ref[...]).astype(BF16)
        h_ref[...] = h
        acc_ref[...] = jnp.zeros(acc_ref.shape, F32)
        lanef = lane.astype(F32)
        logits = jnp.where(lane < ne, _dot(h, wr_ref[...]), NEG)
        v1 = jnp.max(logits, axis=-1, keepdims=True)
        i1 = jnp.min(jnp.where(logits == v1, lanef, 1e9), axis=-1, keepdims=True)
        rest = jnp.where(lanef == i1, NEG, logits)
        v2 = jnp.max(rest, axis=-1, keepdims=True)
        i2 = jnp.min(jnp.where(rest == v2, lanef, 1e9), axis=-1, keepdims=True)
        e2 = jnp.exp(v2 - v1)
        gate_ref[...] = jnp.where(lanef == i1, 1.0 / (1.0 + e2), 0.0) + jnp.where(lanef == i2, e2 / (1.0 + e2), 0.0)

    h = h_ref[...]
    act = jax.nn.silu(_dot(h, wg_ref[...])) * _dot(h, wu_ref[...])
    ge = jnp.sum(jnp.where(lane == e, gate_ref[...], 0.0), axis=-1, keepdims=True)
    acc_ref[...] += ge * _dot(act.astype(BF16), wd_ref[...])

    @pl.when((e == ne - 1) & (f == nf - 1))
    def _():
        y = x_ref[...] + acc_ref[...]
        o_ref[...] = _rms(y, gf_ref[...]) if final else y


def _moe(x, g, gfinal, router, wgu, wd, *, final, tm=512, tf=512):
    t, d = x.shape
    tm = tm if t % tm == 0 else 256
    ne, dff = wd.shape[0], wd.shape[1]
    nf = dff // tf
    return pl.pallas_call(
        functools.partial(_moe_kernel, ne=ne, nf=nf, final=final),
        grid=(t // tm, ne, nf),
        in_specs=[pl.BlockSpec((tm, d), lambda i, e, f: (i, 0)), pl.BlockSpec((1, d), lambda i, e, f: (0, 0)),
                  pl.BlockSpec((1, d), lambda i, e, f: (0, 0)), pl.BlockSpec((d, LANE), lambda i, e, f: (0, 0)),
                  pl.BlockSpec((None, d, tf), lambda i, e, f: (e, 0, f)),
                  pl.BlockSpec((None, d, tf), lambda i, e, f: (e, 0, nf + f)),
                  pl.BlockSpec((None, tf, d), lambda i, e, f: (e, f, 0))],
        out_specs=pl.BlockSpec((tm, d), lambda i, e, f: (i, 0)),
        out_shape=jax.ShapeDtypeStruct((t, d), F32),
        scratch_shapes=[pltpu.VMEM((tm, d), BF16), pltpu.VMEM((tm, LANE), F32), pltpu.VMEM((tm, d), F32)],
        compiler_params=_params(("parallel", "arbitrary", "arbitrary")),
        name="moe",
    )(x, g, gfinal, router, wgu, wgu, wd)


def _rope_tables(pos):
    posf = pos.astype(F32)[:, None]

    def cs(rot, theta):
        inv = 1.0 / (theta ** (jnp.arange(0, rot, 2, dtype=F32) / rot))
        ang = posf * inv[None, :]
        return jnp.cos(ang), jnp.sin(ang)

    t = pos.shape[0]
    cp, sp = cs(2 * ROT_HALF, ROPE_THETA)
    one, zero = jnp.ones((t, HEAD_DIM - 2 * ROT_HALF), F32), jnp.zeros((t, HEAD_DIM - 2 * ROT_HALF), F32)
    z8 = jnp.zeros((t, ROT_HALF), F32)
    cd = jnp.tile(jnp.concatenate([cp, cp, one], 1), (1, 2))
    s1d = jnp.tile(jnp.concatenate([-sp, z8, zero], 1), (1, 2))
    s2d = jnp.tile(jnp.concatenate([z8, sp, zero], 1), (1, 2))
    cmm, smm = cs(MLA_ROPE, MLA_THETA)
    hm = MLA_ROPE // 2
    zm, onem = jnp.zeros((t, hm), F32), jnp.ones((t, LANE - MLA_ROPE), F32)
    zrest = jnp.zeros((t, LANE - MLA_ROPE), F32)
    cm = jnp.concatenate([cmm, cmm, onem], 1)
    s1m = jnp.concatenate([-smm, zm, zrest], 1)
    s2m = jnp.concatenate([zm, smm, zrest], 1)
    return cd, s1d, s2d, cm, s1m, s2m


def _pad_heads(w, offsets):
    out = jnp.zeros((w.shape[0], N_HEADS, LANE), w.dtype)
    for hh in range(N_HEADS):
        out = out.at[:, hh, offsets[hh]:offsets[hh] + HEAD_DIM].set(w[:, hh])
    return out.reshape(w.shape[0], N_HEADS * LANE)


def _layer_weights(l, w_in, mla_q_norm, mla_kv_norm, mla_w_uq, mla_w_uk, mla_w_uv, nsa_cmp_pos, nsa_cmp_w1,
                   nsa_cmp_w2, w_branch, w_out):
    d = w_in.shape[1]
    w = w_in[l]
    sizes = (MLA_Q_LORA, MLA_KV_LORA, MLA_ROPE, N_HEADS * HEAD_DIM, 2 * HEAD_DIM, 2 * HEAD_DIM, 2 * HEAD_DIM,
             3 * N_HEADS, N_HEADS * HEAD_DIM, 2 * MOBA_KVH * HEAD_DIM, 3 * d)
    pts = [int(v) for v in np.cumsum(sizes)[:-1]]
    cq, ckv, kr, qb, kvc, kvs, kvw, gb, qc, kvm, gm = jnp.split(w, pts, axis=1)
    padl = lambda a: jnp.pad(a, ((0, 0), (0, LANE - a.shape[1])))
    moba_off = [HEAD_DIM * (hh // (N_HEADS // MOBA_KVH)) for hh in range(N_HEADS)]
    wp = jnp.concatenate([cq, ckv, padl(kr), _pad_heads(qb.reshape(d, N_HEADS, HEAD_DIM), [0] * N_HEADS), kvc, kvs,
                          kvw, padl(gb), _pad_heads(qc.reshape(d, N_HEADS, HEAD_DIM), moba_off), kvm], axis=1)
    assert wp.shape[1] == C_END
    eye_h, eye_2 = jnp.eye(N_HEADS, dtype=F32), jnp.eye(2, dtype=F32)
    uq = mla_w_uq[l]
    wuqr = jnp.pad(uq[:, :, MLA_NOPE:], ((0, 0), (0, 0), (0, LANE - MLA_ROPE))).reshape(MLA_Q_LORA, N_HEADS * LANE)
    w1 = nsa_cmp_w1[l]
    half = lambda a: jnp.einsum('cbde,cg->bcdge', a, eye_2).reshape(CMP_STRIDE * LANE, 2 * CMP_HIDDEN)
    pos = nsa_cmp_pos[l].reshape(CMP_BLOCK, LANE)
    pa, pb = pos[:CMP_STRIDE], pos[CMP_STRIDE:]
    wb = w_branch[l]
    wb1 = jnp.pad(wb[1].reshape(N_HEADS, HEAD_DIM, d), ((0, 0), (HEAD_DIM, 0), (0, 0))).reshape(N_HEADS * LANE, d)
    wb2 = _pad_heads(wb[2].reshape(N_HEADS, HEAD_DIM, d).transpose(2, 0, 1), moba_off).T
    bf = lambda a: a.astype(BF16)
    return dict(
        wp=bf(wp), wgm=bf(gm), qn=mla_q_norm[l][None], kvn=mla_kv_norm[l][None],
        wuqn=bf(uq[:, :, :MLA_NOPE].reshape(MLA_Q_LORA, N_HEADS * MLA_NOPE)), wuqr=bf(wuqr),
        wuk=bf(jnp.einsum('chd,hg->hdgc', mla_w_uk[l], eye_h).reshape(N_HEADS * MLA_NOPE, N_HEADS * MLA_KV_LORA)),
        wuv=bf(jnp.einsum('chv,hg->hcgv', mla_w_uv[l], eye_h).reshape(N_HEADS * MLA_KV_LORA, N_HEADS * HEAD_DIM)),
        w1a=bf(half(w1[:, :CMP_STRIDE])), w1b=bf(half(w1[:, CMP_STRIDE:])),
        w2=bf(jnp.einsum('ced,cg->cegd', nsa_cmp_w2[l], eye_2).reshape(2 * CMP_HIDDEN, LANE)),
        pa=pa.reshape(1, CMP_STRIDE * LANE), pb=pb.reshape(1, CMP_STRIDE * LANE),
        pat=jnp.tile(pa.T, (1, 2 * PAGE // CMP_STRIDE)), pbt=jnp.tile(pb.T, (1, 2 * PAGE // CMP_STRIDE)),
        wb0=bf(wb[0]), wb1=bf(wb1), wb2=bf(wb2), wout=bf(w_out[l]))


def kernel(x_prompt, x_sample, cache_mla, cache_nsa_cmp, cache_nsa_sel, state_nsa_win, cache_moba, page_table, norm_mix, w_in, mla_q_norm, mla_kv_norm, mla_w_uq, mla_w_uk, mla_w_uv, nsa_cmp_pos, nsa_cmp_w1, nsa_cmp_w2, w_branch, w_out, norm_ffn, ffn_w_gate_up, ffn_w_down, moe_router, moe_w_gate_up, moe_w_down, norm_final):
    nb, seq, d = x_prompt.shape
    ns, nn, _ = x_sample.shape
    depth = w_in.shape[0]
    n_pages = page_table.shape[1]
    past_len = n_pages * PAGE
    tp, ts = nb * seq, ns * nn
    assert nn == 4 and ts % 256 == 0 and seq % 512 == 0

    x = jnp.concatenate([x_prompt.reshape(tp, d), x_sample.reshape(ts, d)], axis=0)
    pos = jnp.concatenate([jnp.tile(jnp.arange(seq), nb), jnp.tile(past_len + jnp.arange(nn), ns)])
    tabs = _rope_tables(pos)
    pt = page_table.reshape(-1)
    npool = cache_mla.shape[1]
    mla_t = jnp.transpose(cache_mla, (0, 1, 3, 2))
    cmp_t = jnp.transpose(cache_nsa_cmp, (0, 1, 3, 4, 2)).reshape(depth, npool, LANE, PAGE)
    sel_t = jnp.transpose(cache_nsa_sel, (0, 1, 3, 4, 2)).reshape(depth, npool, LANE, PAGE)
    moba_t = jnp.transpose(cache_moba, (0, 1, 3, 4, 5, 2)).reshape(depth, npool, 2 * LANE, PAGE)
    win_t = jnp.transpose(state_nsa_win, (0, 1, 3, 4, 2)).reshape(depth, ns, LANE, state_nsa_win.shape[2])

    def stack_q(a):
        return a[tp:].reshape(ns, nn * N_HEADS, a.shape[1] // N_HEADS)

    def new_rows(a):
        return jnp.pad(a[tp:].reshape(ns, nn, a.shape[1]), ((0, 0), (0, 16 - nn), (0, 0)))

    outs = [[] for _ in range(10)]
    for l in range(depth):
        wl = _layer_weights(l, w_in, mla_q_norm, mla_kv_norm, mla_w_uq, mla_w_uk, mla_w_uv, nsa_cmp_pos,
                            nsa_cmp_w1, nsa_cmp_w2, w_branch, w_out)
        g_mix = norm_mix[l][None]
        (qmla, mlaf, mlab, qnsa, cmpf, self_, selb, winf, winb, gates, qmoba, mobaf, mobab) = _project(
            x, g_mix, wl, tabs, 256)

        olat_p = _prompt_attention(qmla, mlab, None, mode='mla', n_seq=nb, seq=seq)
        ocmp_p, selmask = _nsa_select_prompt(cmpf, qnsa, wl, n_seq=nb, seq=seq)
        osel_p = _prompt_attention(qnsa, selb, selmask, mode='sel', n_seq=nb, seq=seq)
        owin_p = _prompt_attention(qnsa, winb, None, mode='win', n_seq=nb, seq=seq)
        mobasel = _moba_select_prompt(mobaf, qmoba, n_seq=nb, seq=seq)
        omoba_p = _prompt_attention(qmoba, mobab, mobasel, mode='moba', n_seq=nb, seq=seq)

        olat_s = _mla_sample(pt, stack_q(qmla), new_rows(mlab), mla_t, l)
        gs = gates[tp:, 0:3 * N_HEADS].reshape(ns, nn, 3, N_HEADS).transpose(0, 1, 3, 2).reshape(ns, nn * N_HEADS, 3)
        gs = jnp.pad(gs, ((0, 0), (0, 0), (0, LANE - 3)))
        onsa_s = _nsa_sample(pt, stack_q(qnsa), new_rows(selb), new_rows(winb), win_t, gs, cmp_t, sel_t, wl, l)
        omoba_s = _moba_sample(pt, stack_q(qmoba), new_rows(mobab), moba_t, l)

        unstack = lambda o: o.reshape(ts, N_HEADS * LANE)
        olat = jnp.concatenate([olat_p, unstack(olat_s)], axis=0)
        ocmp = jnp.concatenate([ocmp_p, unstack(onsa_s)], axis=0)
        zeros_s = jnp.zeros((ts, N_HEADS * LANE), F32)
        osel = jnp.concatenate([osel_p, zeros_s], axis=0)
        owin = jnp.concatenate([owin_p, zeros_s], axis=0)
        unit = jnp.concatenate([jnp.ones((ts, N_HEADS), F32), jnp.zeros((ts, LANE - N_HEADS), F32)], axis=1)
        gates_m = jnp.concatenate([gates[:tp], unit], axis=0)
        omoba = jnp.concatenate([omoba_p, unstack(omoba_s)], axis=0)

        x = _merge(x, g_mix, olat, ocmp, osel, owin, gates_m, omoba, wl, 256)
        last = l == depth - 1
        g_ffn, g_fin = norm_ffn[l][None], norm_final[None]
        if l % 2 == 0:
            x = _ffn(x, g_ffn, g_fin, ffn_w_gate_up[l // 2].astype(BF16), ffn_w_down[l // 2].astype(BF16), final=last)
        else:
            router = jnp.pad(moe_router[l // 2], ((0, 0), (0, LANE - N_EXPERTS))).astype(BF16)
            x = _moe(x, g_ffn, g_fin, router, moe_w_gate_up[l // 2].astype(BF16), moe_w_down[l // 2].astype(BF16),
                     final=last)

        win_prompt = winf[:tp].reshape(nb, seq, 2, HEAD_DIM)[:, seq - min(NSA_WINDOW, seq):]
        win_new = winf[tp:].reshape(ns, nn, 2, HEAD_DIM)
        rows = (mlaf[:tp].reshape(nb, seq, MLA_W), mlaf[tp:].reshape(ns, nn, MLA_W),
                cmpf[:tp].reshape(nb, seq, 2, HEAD_DIM), cmpf[tp:].reshape(ns, nn, 2, HEAD_DIM),
                self_[:tp].reshape(nb, seq, 2, HEAD_DIM), self_[tp:].reshape(ns, nn, 2, HEAD_DIM),
                win_prompt, jnp.concatenate([state_nsa_win[l], win_new], axis=1)[:, nn:],
                mobaf[:tp].reshape(nb, seq, 2, MOBA_KVH, HEAD_DIM), mobaf[tp:].reshape(ns, nn, 2, MOBA_KVH, HEAD_DIM))
        for acc, r in zip(outs, rows):
            acc.append(r)

    y_prompt = x[:tp].reshape(nb, seq, d)
    y_sample = x[tp:].reshape(ns, nn, d)
    return (y_prompt, y_sample) + tuple(jnp.stack(o) for o in outs)
```
